```python
import math
import jax, jax.numpy as jnp
from jax import lax
import numpy as np

D_MODEL = 1024
BATCH = 16
SEQ = 256
DEPTH = 2
DEC_BATCH = 8
DEC_SEQ = 4096
PAST_LEN = 512

GRID_W = 64
HEAD_DIM = 64
V_DIM = 2 * HEAD_DIM
ATTN_WIDTH = D_MODEL // 2
N_HEADS_A = ATTN_WIDTH // V_DIM
POOL_WIDTH = D_MODEL // 4
POOL_WINDOWS = (2, 4, 8, 16)
POOL_GROUPS = len(POOL_WINDOWS)
POOL_GROUP_DIM = POOL_WIDTH // POOL_GROUPS
FOURIER_WIDTH = D_MODEL // 4
FOURIER_GROUPS = 4
FOURIER_GROUP_DIM = FOURIER_WIDTH // FOURIER_GROUPS
MIX_WIDTH = ATTN_WIDTH + POOL_WIDTH + FOURIER_WIDTH
IN_WIDTH = 3 * ATTN_WIDTH + POOL_WIDTH + FOURIER_WIDTH
N_EXPERTS = 32
TOP_K = 4
D_FF = D_MODEL
SWIGLU_LIMIT = 7.0
SWIGLU_ALPHA = 1.702
ROPE_THETA = 10000.0
RMS_EPS = 1e-6
Q_BLOCK = 128
MOE_BLOCK = 256

kernel_name = "hybrid_diff_pool_fourier_moe_dit_step"


def rmsnorm(x, g):
    xf = x.astype(jnp.float32)
    y = xf * lax.rsqrt(jnp.mean(xf * xf, axis=-1, keepdims=True) + RMS_EPS)
    return (y * g.astype(jnp.float32)).astype(x.dtype)


def modulation(cond, ada_w, ada_b):
    m = jax.nn.silu(cond) @ ada_w + ada_b
    m = jnp.expand_dims(m, -2)
    return jnp.split(m, 6, axis=-1)


def lambda_init_fn(layer_idx):
    return 0.8 - 0.6 * math.exp(-0.3 * layer_idx)


def axial_rope(x):
    T = x.shape[1]
    n_rows = T // GRID_W
    rows = jnp.repeat(jnp.arange(n_rows), GRID_W)
    cols = jnp.tile(jnp.arange(GRID_W), n_rows)
    half = HEAD_DIM // 2
    quarter = HEAD_DIM // 4
    inv_freq = ROPE_THETA ** (-jnp.arange(quarter, dtype=jnp.float32) / quarter)

    def rot(xp, pos):
        ang = pos.astype(jnp.float32)[:, None] * inv_freq[None, :]
        cos = jnp.cos(ang)[None, :, None, None, :]
        sin = jnp.sin(ang)[None, :, None, None, :]
        x1, x2 = xp[..., :quarter], xp[..., quarter:]
        return jnp.concatenate([x1 * cos - x2 * sin, x1 * sin + x2 * cos], axis=-1)

    xf = x.astype(jnp.float32)
    out = jnp.concatenate([rot(xf[..., :half], rows), rot(xf[..., half:], cols)], axis=-1)
    return out.astype(x.dtype)


def diff_attention(q, k, v, lam):
    B, T = q.shape[:2]
    nb = T // Q_BLOCK
    qb = q.reshape(B, nb, Q_BLOCK, N_HEADS_A, 2, HEAD_DIM).swapaxes(0, 1)
    kf = k.astype(jnp.float32)
    vf = v.astype(jnp.float32)
    scale = HEAD_DIM ** -0.5

    def block(qi):
        s = jnp.einsum('bqhmd,bkhmd->bhmqk', qi.astype(jnp.float32), kf) * scale
        p = jax.nn.softmax(s, axis=-1)
        a = p[:, :, 0] - lam * p[:, :, 1]
        return jnp.einsum('bhqk,bkhe->bqhe', a, vf)

    o = lax.map(block, qb)
    return o.swapaxes(0, 1).reshape(B, T, N_HEADS_A, V_DIM)


def multiscale_pool(u, pool_w, pool_scale):
    B, T, _ = u.shape
    uf = u.astype(jnp.float32).reshape(B, T, POOL_GROUPS, POOL_GROUP_DIM)
    t = jnp.arange(T)
    outs = []
    for g, w in enumerate(POOL_WINDOWS):
        ug = uf[:, :, g]
        cs = jnp.pad(jnp.cumsum(ug, axis=1), ((0, 0), (1, 0), (0, 0)))
        lo = jnp.clip(t - w // 2, 0, T)
        hi = jnp.clip(t + w // 2, 0, T)
        mean = (cs[:, hi] - cs[:, lo]) / (hi - lo).astype(jnp.float32)[None, :, None]
        outs.append(mean - ug)
    y = jnp.stack(outs, axis=2)
    y = jnp.einsum('btgc,gcd->btgd', y, pool_w.astype(jnp.float32)).reshape(B, T, POOL_WIDTH)
    return (y * pool_scale.astype(jnp.float32)).astype(u.dtype)


def fourier_mix(u, fourier_w):
    B, T, _ = u.shape
    uf = u.astype(jnp.float32).reshape(B, T, FOURIER_GROUPS, FOURIER_GROUP_DIM)
    f = jnp.fft.fft2(uf, axes=(1, 3), norm='ortho').real
    y = jnp.einsum('btgc,gcd->btgd', f, fourier_w.astype(jnp.float32))
    return y.reshape(B, T, FOURIER_WIDTH).astype(u.dtype)


def mixer_sublayer(h, w_in, q_g, k_g, lam_qk, subln_g, lam_init, pool_w, pool_scale, fourier_w, w_out,
                   ctx_k, ctx_v):
    B, T, _ = h.shape
    proj = h @ w_in
    q, k, v, u_pool, u_four = jnp.split(
        proj, [ATTN_WIDTH, 2 * ATTN_WIDTH, 3 * ATTN_WIDTH, 3 * ATTN_WIDTH + POOL_WIDTH], axis=-1)
    q = rmsnorm(q.reshape(B, T, N_HEADS_A, 2, HEAD_DIM), q_g)
    k = rmsnorm(k.reshape(B, T, N_HEADS_A, 2, HEAD_DIM), k_g)
    v = v.reshape(B, T, N_HEADS_A, V_DIM)
    if ctx_k is None:
        keys, vals = k, v
        q_use = q
    else:
        q_use = axial_rope(q)
        keys = jnp.concatenate([axial_rope(k), ctx_k.astype(k.dtype)], axis=1)
        vals = jnp.concatenate([v, ctx_v.astype(v.dtype)], axis=1)
    lf = lam_qk.astype(jnp.float32)
    lam = jnp.exp(jnp.sum(lf[0] * lf[1])) - jnp.exp(jnp.sum(lf[2] * lf[3])) + lam_init
    o = diff_attention(q_use, keys, vals, lam)
    o = rmsnorm(o, subln_g) * (1.0 - lam_init)
    attn_out = o.reshape(B, T, ATTN_WIDTH).astype(h.dtype)
    mixed = jnp.concatenate(
        [attn_out, multiscale_pool(u_pool, pool_w, pool_scale), fourier_mix(u_four, fourier_w)], axis=-1)
    return mixed @ w_out, k, v


def moe_ffn(h, router_w, router_b, w_gu, b_gu, w_down, b_down):
    B, T, D = h.shape
    x = h.reshape(-1, D)
    N = x.shape[0]
    logits = x.astype(jnp.float32) @ router_w.astype(jnp.float32) + router_b.astype(jnp.float32)
    top_v, top_i = lax.top_k(logits, TOP_K)
    gates = jax.nn.softmax(top_v, axis=-1)
    n_assign = N * TOP_K
    flat_e = top_i.reshape(-1)
    order = jnp.argsort(flat_e)
    sorted_e = flat_e[order]
    counts = jnp.bincount(flat_e, length=N_EXPERTS)
    padded = (counts + MOE_BLOCK - 1) // MOE_BLOCK * MOE_BLOCK
    pad_end = jnp.cumsum(padded)
    pad_start = pad_end - padded
    start = jnp.cumsum(counts) - counts
    rank = jnp.arange(n_assign) - start[sorted_e]
    dest = pad_start[sorted_e] + rank
    n_blocks = -(-n_assign // MOE_BLOCK) + N_EXPERTS
    n_rows = n_blocks * MOE_BLOCK
    row_token = jnp.full((n_rows,), N, jnp.int32).at[dest].set((order // TOP_K).astype(jnp.int32))
    row_gate = jnp.zeros((n_rows,), jnp.float32).at[dest].set(gates.reshape(-1)[order])
    block_expert = jnp.minimum(
        jnp.searchsorted(pad_end, jnp.arange(n_blocks) * MOE_BLOCK, side='right'), N_EXPERTS - 1)
    x_rows = jnp.concatenate([x, jnp.zeros((1, D), x.dtype)], axis=0)[row_token]
    x_rows = x_rows.reshape(n_blocks, MOE_BLOCK, D)

    def expert_block(args):
        xb, e = args
        gu = xb @ w_gu[e] + b_gu[e]
        glu, lin = jnp.split(gu, 2, axis=-1)
        glu = jnp.minimum(glu, SWIGLU_LIMIT)
        lin = jnp.clip(lin, -SWIGLU_LIMIT, SWIGLU_LIMIT)
        act = glu * jax.nn.sigmoid(SWIGLU_ALPHA * glu) * (lin + 1.0)
        return act @ w_down[e] + b_down[e]

    y_rows = lax.map(expert_block, (x_rows, block_expert)).reshape(n_rows, D)
    y_rows = y_rows * row_gate[:, None].astype(y_rows.dtype)
    out = jax.ops.segment_sum(y_rows, row_token, num_segments=N + 1)[:N]
    return out.reshape(B, T, D).astype(h.dtype)


def trunk_layer(x, mod, lam_init, lw, ctx_k, ctx_v):
    (n1, n2, w_in, q_g, k_g, lam_qk, subln_g, pool_w, pool_scale, fourier_w, w_out,
     router_w, router_b, w_gu, b_gu, w_down, b_down) = lw
    shift1, scale1, gate1, shift2, scale2, gate2 = mod
    h = rmsnorm(x, n1) * (1.0 + scale1) + shift1
    mix, k, v = mixer_sublayer(h, w_in, q_g, k_g, lam_qk, subln_g, lam_init, pool_w, pool_scale,
                               fourier_w, w_out, ctx_k, ctx_v)
    x = x + gate1 * mix
    h = rmsnorm(x, n2) * (1.0 + scale2) + shift2
    x = x + gate2 * moe_ffn(h, router_w, router_b, w_gu, b_gu, w_down, b_down)
    return x, k, v


def setup_inputs(seed: int = 0) -> dict:
    key = jax.random.key(seed)
    ks = jax.random.split(key, 26)
    f32 = jnp.float32
    D = D_MODEL

    def nrm(k, shape, s):
        return jax.random.normal(k, shape, f32) * s

    return {
        'x_prompt': nrm(ks[0], (BATCH, SEQ, D), 1.0),
        'x_sample': nrm(ks[1], (DEC_BATCH, DEC_SEQ, D), 1.0),
        'c': nrm(ks[2], (DEC_BATCH, D), 1.0),
        'c_ctx': nrm(ks[3], (D,), 1.0),
        'cache_k': nrm(ks[4], (DEC_BATCH, DEPTH, PAST_LEN, N_HEADS_A, 2, HEAD_DIM), 1.0),
        'cache_v': nrm(ks[5], (DEC_BATCH, DEPTH, PAST_LEN, N_HEADS_A, V_DIM), 1.0),
        'ada_w': nrm(ks[6], (DEPTH, D, 6 * D), 0.5 * D ** -0.5),
        'ada_b': nrm(ks[7], (DEPTH, 6 * D), 0.02),
        'norm1_g': 1.0 + nrm(ks[8], (DEPTH, D), 0.05),
        'norm2_g': 1.0 + nrm(ks[9], (DEPTH, D), 0.05),
        'w_in': nrm(ks[10], (DEPTH, D, IN_WIDTH), D ** -0.5),
        'q_norm_g': 1.0 + nrm(ks[11], (DEPTH, HEAD_DIM), 0.05),
        'k_norm_g': 1.0 + nrm(ks[12], (DEPTH, HEAD_DIM), 0.05),
        'lambda_qk': nrm(ks[13], (DEPTH, 4, HEAD_DIM), 0.1),
        'subln_g': 1.0 + nrm(ks[14], (DEPTH, V_DIM), 0.05),
        'pool_w': nrm(ks[15], (DEPTH, POOL_GROUPS, POOL_GROUP_DIM, POOL_GROUP_DIM), POOL_GROUP_DIM ** -0.5),
        'pool_scale': 1.0 + nrm(ks[16], (DEPTH, POOL_WIDTH), 0.05),
        'fourier_w': nrm(ks[17], (DEPTH, FOURIER_GROUPS, FOURIER_GROUP_DIM, FOURIER_GROUP_DIM),
                         FOURIER_GROUP_DIM ** -0.5),
        'w_out': nrm(ks[18], (DEPTH, MIX_WIDTH, D), MIX_WIDTH ** -0.5),
        'router_w': nrm(ks[19], (DEPTH, D, N_EXPERTS), D ** -0.5),
        'router_b': nrm(ks[20], (DEPTH, N_EXPERTS), 0.01),
        'w_gate_up': nrm(ks[21], (DEPTH, N_EXPERTS, D, 2 * D_FF), D ** -0.5),
        'b_gate_up': nrm(ks[22], (DEPTH, N_EXPERTS, 2 * D_FF), 0.01),
        'w_down': nrm(ks[23], (DEPTH, N_EXPERTS, D_FF, D), D_FF ** -0.5),
        'b_down': nrm(ks[24], (DEPTH, N_EXPERTS, D), 0.01),
    }


def reference(x_prompt, x_sample, c, c_ctx, cache_k, cache_v, ada_w, ada_b, norm1_g, norm2_g, w_in,
              q_norm_g, k_norm_g, lambda_qk, subln_g, pool_w, pool_scale, fourier_w, w_out,
              router_w, router_b, w_gate_up, b_gate_up, w_down, b_down):
    stacked = (norm1_g, norm2_g, w_in, q_norm_g, k_norm_g, lambda_qk, subln_g, pool_w, pool_scale,
               fourier_w, w_out, router_w, router_b, w_gate_up, b_gate_up, w_down, b_down)
    layers = [tuple(w[l] for w in stacked) for l in range(DEPTH)]

    y_prompt = x_prompt
    new_k, new_v = [], []
    for l in range(DEPTH):
        mod_ctx = modulation(c_ctx, ada_w[l], ada_b[l])
        y_prompt, k_l, v_l = trunk_layer(y_prompt, mod_ctx, lambda_init_fn(l), layers[l], None, None)
        new_k.append(k_l)
        new_v.append(v_l)

    y_sample = x_sample
    for l in range(DEPTH):
        mod = modulation(c, ada_w[l], ada_b[l])
        y_sample, _, _ = trunk_layer(y_sample, mod, lambda_init_fn(l), layers[l],
                                     cache_k[:, l], cache_v[:, l])

    new_cache_k = jnp.stack(new_k, axis=1)
    new_cache_v = jnp.stack(new_v, axis=1)
    return (y_prompt, y_sample, new_cache_k, new_cache_v)
```

```python
import functools
import math

import numpy as np
import jax
import jax.numpy as jnp
from jax import lax
from jax.experimental import pallas as pl
from jax.experimental.pallas import tpu as pltpu

F32 = jnp.float32
BF16 = jnp.bfloat16

GRID_W = 64
HEAD_DIM = 64
V_DIM = 2 * HEAD_DIM
N_HEADS = 4
ATTN_W = N_HEADS * V_DIM
POOL_WINDOWS = (2, 4, 8, 16)
POOL_GROUP = 64
POOL_W = 256
FOUR_GROUPS = 4
FOUR_GROUP = 64
FOUR_W = 256
N_EXPERTS = 32
TOP_K = 4
SWIGLU_LIMIT = 7.0
SWIGLU_ALPHA = 1.702
ROPE_THETA = 10000.0
RMS_EPS = 1e-6

LANES = 128
TOKEN_TILE = 256
POOL_HALO = 8
KV_CHUNK = 256
EXPERT_BLOCK = 256
DFT_ROW_TILE = 512
MOD_ROWS = 16
VMEM_LIMIT = 56 * 1024 * 1024


def _dot(a, b):
    return jnp.dot(a, b, preferred_element_type=F32)


def _split_bf16(x):
    hi = x.astype(BF16)
    lo = (x - hi.astype(F32)).astype(BF16)
    return hi, lo


def _cparams(sem):
    return pltpu.CompilerParams(dimension_semantics=sem, vmem_limit_bytes=VMEM_LIMIT)


def _mod_kernel(cond_ref, w_ref, b_ref, o_ref):
    c = cond_ref[...]
    a = c * jax.nn.sigmoid(c)
    ah, al = _split_bf16(a)
    wh, wl = _split_bf16(w_ref[0])
    o_ref[0] = _dot(ah, wh) + _dot(ah, wl) + _dot(al, wh) + b_ref[0]


def modulation_all(cond, ada_w, ada_b):
    depth, d, d6 = ada_w.shape
    tn = d6 // 4
    return pl.pallas_call(
        _mod_kernel,
        out_shape=jax.ShapeDtypeStruct((depth, MOD_ROWS, d6), F32),
        grid=(depth, d6 // tn),
        in_specs=[
            pl.BlockSpec((MOD_ROWS, d), lambda l, j: (0, 0)),
            pl.BlockSpec((1, d, tn), lambda l, j: (l, 0, j)),
            pl.BlockSpec((1, 1, tn), lambda l, j: (l, 0, j)),
        ],
        out_specs=pl.BlockSpec((1, MOD_ROWS, tn), lambda l, j: (l, 0, j)),
        compiler_params=_cparams(("parallel", "parallel")),
        name="modulation",
    )(cond, ada_w, ada_b.reshape(depth, 1, d6))


def _rms_mod(x, g, scale, shift):
    ms = jnp.mean(x * x, axis=-1, keepdims=True)
    return x * lax.rsqrt(ms + RMS_EPS) * g * (1.0 + scale) + shift


def _pre_mixer_kernel(tmod_ref, x_ref, g_ref, mod_ref, w_ref, ab_ref, proj_ref, za_ref, zb_ref):
    del tmod_ref
    h = _rms_mod(x_ref[...], g_ref[...], mod_ref[0, 1], mod_ref[0, 0])
    proj = _dot(h.astype(BF16), w_ref[...])
    nproj = proj_ref.shape[1]
    proj_ref[...] = proj[:, :nproj]
    z = _dot(proj[:, nproj:].astype(BF16), ab_ref[...])
    za_ref[...] = z[:, :FOUR_W].astype(BF16)
    zb_ref[...] = z[:, FOUR_W:].astype(BF16)


def pre_mixer(x, tile_mod, n1, mods, w_in_b, ab_b):
    n, d = x.shape
    in_w = w_in_b.shape[1]
    nproj = in_w - FOUR_W
    tm = TOKEN_TILE
    return pl.pallas_call(
        _pre_mixer_kernel,
        out_shape=(jax.ShapeDtypeStruct((n, nproj), F32),
                   jax.ShapeDtypeStruct((n, FOUR_W), BF16),
                   jax.ShapeDtypeStruct((n, FOUR_W), BF16)),
        grid_spec=pltpu.PrefetchScalarGridSpec(
            num_scalar_prefetch=1,
            grid=(n // tm,),
            in_specs=[
                pl.BlockSpec((tm, d), lambda i, tmod: (i, 0)),
                pl.BlockSpec((1, d), lambda i, tmod: (0, 0)),
                pl.BlockSpec((1, 6, 1, d), lambda i, tmod: (tmod[i], 0, 0, 0)),
                pl.BlockSpec((d, in_w), lambda i, tmod: (0, 0)),
                pl.BlockSpec((FOUR_W, 2 * FOUR_W), lambda i, tmod: (0, 0)),
            ],
            out_specs=(pl.BlockSpec((tm, nproj), lambda i, tmod: (i, 0)),
                       pl.BlockSpec((tm, FOUR_W), lambda i, tmod: (i, 0)),
                       pl.BlockSpec((tm, FOUR_W), lambda i, tmod: (i, 0))),
        ),
        compiler_params=_cparams(("parallel",)),
        name="pre_mixer",
    )(tile_mod, x, n1.reshape(1, d), mods, w_in_b, ab_b)


def _group_rms(x, gmat, g):
    sh, sl = _split_bf16(x * x)
    ss = _dot(sh, gmat) + _dot(sl, gmat)
    return x * lax.rsqrt(ss * (1.0 / HEAD_DIM) + RMS_EPS) * g


def _rope(y, cos, sin_signed):
    w = y.shape[1]
    q = HEAD_DIM // 4
    lane = lax.broadcasted_iota(jnp.int32, y.shape, 1)
    partner = jnp.where(lane % (2 * q) < q, pltpu.roll(y, w - q, axis=1), pltpu.roll(y, q, axis=1))
    return y * cos + partner * sin_signed


def _qkv_kernel(q_ref, k_ref, v_ref, gmat_ref, qg_ref, kg_ref, *rest, rope, emit_kn):
    if rope:
        cos_ref, sin_ref, *outs = rest
    else:
        outs = rest
    qt_ref, kk_ref, vt_ref = outs[:3]
    gmat = gmat_ref[...]
    qn = _group_rms(q_ref[...], gmat, qg_ref[...])
    kn = _group_rms(k_ref[...], gmat, kg_ref[...])
    if emit_kn:
        outs[3][0] = kn
    if rope:
        cos = cos_ref[...]
        sin = sin_ref[...]
        qn = _rope(qn, cos, sin)
        kn = _rope(kn, cos, sin)
    qt_ref[0] = (qn * (HEAD_DIM ** -0.5)).T.astype(BF16)
    kk_ref[0] = kn.astype(BF16)
    vt_ref[0, 0] = v_ref[...].T.astype(BF16)


def qkv_prep(proj, row_off, b, t, gmat, qg, kg, cos, sin, *, rope, emit_kn):
    tm = TOKEN_TILE
    nt = t // tm
    off = row_off // tm
    w = ATTN_W
    row = lambda bi, j: (off + bi * nt + j)
    in_specs = [
        pl.BlockSpec((tm, w), lambda bi, j: (row(bi, j), 0)),
        pl.BlockSpec((tm, w), lambda bi, j: (row(bi, j), 1)),
        pl.BlockSpec((tm, w), lambda bi, j: (row(bi, j), 2)),
        pl.BlockSpec((w, w), lambda bi, j: (0, 0)),
        pl.BlockSpec((1, w), lambda bi, j: (0, 0)),
        pl.BlockSpec((1, w), lambda bi, j: (0, 0)),
    ]
    args = [proj, proj, proj, gmat, qg, kg]
    if rope:
        in_specs += [pl.BlockSpec((tm, w), lambda bi, j: (j, 0)),
                     pl.BlockSpec((tm, w), lambda bi, j: (j, 0))]
        args += [cos, sin]
    out_shape = [jax.ShapeDtypeStruct((b, w, t), BF16),
                 jax.ShapeDtypeStruct((b, t, w), BF16),
                 jax.ShapeDtypeStruct((b, nt, w, tm), BF16)]
    out_specs = [pl.BlockSpec((1, w, tm), lambda bi, j: (bi, 0, j)),
                 pl.BlockSpec((1, tm, w), lambda bi, j: (bi, j, 0)),
                 pl.BlockSpec((1, 1, w, tm), lambda bi, j: (bi, j, 0, 0))]
    if emit_kn:
        out_shape.append(jax.ShapeDtypeStruct((b, t, w), F32))
        out_specs.append(pl.BlockSpec((1, tm, w), lambda bi, j: (bi, j, 0)))
    return pl.pallas_call(
        functools.partial(_qkv_kernel, rope=rope, emit_kn=emit_kn),
        out_shape=tuple(out_shape),
        grid=(b, nt),
        in_specs=in_specs,
        out_specs=tuple(out_specs),
        compiler_params=_cparams(("parallel", "parallel")),
        name="qkv_prep_rope" if rope else "qkv_prep",
    )(*args)


def _attn_kernel(lam_ref, qt_ref, k_ref, vt_ref, g_ref, o_ref, m_s, l_s, acc_s, *, nk, lam_init):
    q = qt_ref[0]
    row = lax.broadcasted_iota(jnp.int32, q.shape, 0)
    zero = jnp.zeros_like(q)
    qmaps = (jnp.where(row < HEAD_DIM, q, zero), jnp.where(row >= HEAD_DIM, q, zero))
    m_s[...] = jnp.full(m_s.shape, -jnp.inf, F32)
    l_s[...] = jnp.zeros(l_s.shape, F32)
    acc_s[...] = jnp.zeros(acc_s.shape, F32)

    def body(c, carry):
        k = k_ref[0, pl.ds(pl.multiple_of(c * KV_CHUNK, KV_CHUNK), KV_CHUNK), :]
        v = vt_ref[0, c]
        for m in range(2):
            s = _dot(k, qmaps[m])
            m_prev = m_s[m]
            m_new = jnp.maximum(m_prev, jnp.max(s, axis=0, keepdims=True))
            alpha = jnp.exp(m_prev - m_new)
            p = jnp.exp(s - m_new)
            l_s[m] = alpha * l_s[m] + jnp.sum(p, axis=0, keepdims=True)
            acc_s[m] = alpha * acc_s[m] + _dot(v, p.astype(BF16))
            m_s[m] = m_new
        return carry

    lax.fori_loop(0, nk, body, 0)
    o = acc_s[0] / l_s[0] - lam_ref[0] * (acc_s[1] / l_s[1])
    ms = jnp.mean(o * o, axis=0, keepdims=True)
    o = o * lax.rsqrt(ms + RMS_EPS) * g_ref[...] * (1.0 - lam_init)
    o_ref[...] = o.T.astype(BF16)


def diff_attention(lam, qt, kk, vt, subln_g, lam_init):
    b, w, t = qt.shape
    tk = kk.shape[1]
    nk = tk // KV_CHUNK
    tq = TOKEN_TILE
    nq = t // tq
    hw = V_DIM
    return pl.pallas_call(
        functools.partial(_attn_kernel, nk=nk, lam_init=lam_init),
        out_shape=jax.ShapeDtypeStruct((b * t, w), BF16),
        grid_spec=pltpu.PrefetchScalarGridSpec(
            num_scalar_prefetch=1,
            grid=(b, N_HEADS, nq),
            in_specs=[
                pl.BlockSpec((1, hw, tq), lambda bi, h, qi, lam: (bi, h, qi)),
                pl.BlockSpec((1, tk, hw), lambda bi, h, qi, lam: (bi, 0, h)),
                pl.BlockSpec((1, nk, hw, KV_CHUNK), lambda bi, h, qi, lam: (bi, 0, h, 0)),
                pl.BlockSpec((hw, 1), lambda bi, h, qi, lam: (0, 0)),
            ],
            out_specs=pl.BlockSpec((tq, hw), lambda bi, h, qi, lam: (bi * nq + qi, h)),
            scratch_shapes=[pltpu.VMEM((2, 1, tq), F32),
                            pltpu.VMEM((2, 1, tq), F32),
                            pltpu.VMEM((2, hw, tq), F32)],
        ),
        compiler_params=_cparams(("parallel", "parallel", "parallel")),
        name="diff_attention",
    )(lam, qt, kk, vt, subln_g.reshape(hw, 1))


def _pool_kernel(t0_ref, tlen_ref, prev_ref, cur_ref, next_ref, pw_ref, ps_ref, o_ref):
    i = pl.program_id(0)
    t0 = t0_ref[i]
    tlen = tlen_ref[i]
    tm = cur_ref.shape[0]
    u = cur_ref[...]
    prev = jnp.where(t0 > 0, prev_ref[...], 0.0)
    nxt = jnp.where(t0 + tm < tlen, next_ref[...], 0.0)
    e = jnp.concatenate([prev, u, nxt], axis=0)
    n = e.shape[0]
    w2 = e + pltpu.roll(e, 1, axis=0)
    w4 = pltpu.roll(w2, 1, axis=0) + pltpu.roll(w2, n - 1, axis=0)
    w8 = pltpu.roll(w4, 2, axis=0) + pltpu.roll(w4, n - 2, axis=0)
    w16 = pltpu.roll(w8, 4, axis=0) + pltpu.roll(w8, n - 4, axis=0)
    sums = [w[POOL_HALO:POOL_HALO + tm] for w in (w2, w4, w8, w16)]
    lane = lax.broadcasted_iota(jnp.int32, u.shape, 1)
    grp = lane // POOL_GROUP
    tabs = t0 + lax.broadcasted_iota(jnp.int32, u.shape, 0)
    total = sums[3]
    half = jnp.full(u.shape, POOL_WINDOWS[3] // 2, jnp.int32)
    for gi in (2, 1, 0):
        total = jnp.where(grp == gi, sums[gi], total)
        half = jnp.where(grp == gi, POOL_WINDOWS[gi] // 2, half)
    cnt = jnp.minimum(tabs + half, tlen) - jnp.maximum(tabs - half, 0)
    d = total / cnt.astype(F32) - u
    o_ref[...] = (_dot(d.astype(BF16), pw_ref[...]) * ps_ref[...]).astype(BF16)


def multiscale_pool(proj, tile_t0, tile_len, pw_b, pool_scale):
    n = proj.shape[0]
    tm = TOKEN_TILE
    hb = tm // POOL_HALO
    col = (3 * ATTN_W) // POOL_W
    last = n // POOL_HALO - 1
    return pl.pallas_call(
        _pool_kernel,
        out_shape=jax.ShapeDtypeStruct((n, POOL_W), BF16),
        grid_spec=pltpu.PrefetchScalarGridSpec(
            num_scalar_prefetch=2,
            grid=(n // tm,),
            in_specs=[
                pl.BlockSpec((POOL_HALO, POOL_W), lambda i, a, b: (jnp.maximum(i * hb - 1, 0), col)),
                pl.BlockSpec((tm, POOL_W), lambda i, a, b: (i, col)),
                pl.BlockSpec((POOL_HALO, POOL_W), lambda i, a, b: (jnp.minimum((i + 1) * hb, last), col)),
                pl.BlockSpec((POOL_W, POOL_W), lambda i, a, b: (0, 0)),
                pl.BlockSpec((1, POOL_W), lambda i, a, b: (0, 0)),
            ],
            out_specs=pl.BlockSpec((tm, POOL_W), lambda i, a, b: (i, 0)),
        ),
        compiler_params=_cparams(("parallel",)),
        name="multiscale_pool",
    )(tile_t0, tile_len, proj, proj, proj, pw_b, pool_scale.reshape(1, POOL_W))


def _fourier_kernel(c_ref, s_ref, za_ref, zb_ref, o_ref):
    o_ref[0] = (_dot(c_ref[...], za_ref[0]) - _dot(s_ref[...], zb_ref[0])).astype(BF16)


def fourier_mix(cmat, smat, za, zb):
    b, t, w = za.shape
    tr = min(DFT_ROW_TILE, t)
    return pl.pallas_call(
        _fourier_kernel,
        out_shape=jax.ShapeDtypeStruct((b, t, w), BF16),
        grid=(t // tr, b),
        in_specs=[
            pl.BlockSpec((tr, t), lambda r, bi: (r, 0)),
            pl.BlockSpec((tr, t), lambda r, bi: (r, 0)),
            pl.BlockSpec((1, t, w), lambda r, bi: (bi, 0, 0)),
            pl.BlockSpec((1, t, w), lambda r, bi: (bi, 0, 0)),
        ],
        out_specs=pl.BlockSpec((1, tr, w), lambda r, bi: (bi, r, 0)),
        compiler_params=_cparams(("parallel", "parallel")),
        name="fourier_mix",
    )(cmat, smat, za, zb)


def _post_mixer_kernel(tmod_ref, x_ref, a_ref, p_ref, f_ref, wo_ref, mod_ref, g2_ref, rw_ref, rb_ref,
                       tri_ref, x1_ref, h2_ref, topi_ref, rank_ref, gate_ref, cnt_ref, carry_s):
    del tmod_ref
    i = pl.program_id(0)

    @pl.when(i == 0)
    def _():
        carry_s[...] = jnp.zeros(carry_s.shape, F32)

    mixed = jnp.concatenate([a_ref[...], p_ref[...], f_ref[...]], axis=1)
    x1 = x_ref[...] + mod_ref[0, 2] * _dot(mixed, wo_ref[...])
    x1_ref[...] = x1
    h2 = _rms_mod(x1, g2_ref[...], mod_ref[0, 4], mod_ref[0, 3])
    hh, hl = _split_bf16(h2)
    h2_ref[...] = hh

    rw = rw_ref[...]
    rwh, rwl = _split_bf16(rw)
    logits = _dot(hh, rwh) + _dot(hh, rwl) + _dot(hl, rwh) + rb_ref[...]
    lt = logits.T
    eidx = lax.broadcasted_iota(jnp.int32, lt.shape, 0).astype(F32)
    work = lt
    vals, idxs, hots = [], [], []
    for _ in range(TOP_K):
        mx = jnp.max(work, axis=0, keepdims=True)
        ix = jnp.min(jnp.where(work == mx, eidx, float(LANES)), axis=0, keepdims=True)
        hot = eidx == ix
        work = jnp.where(hot, -jnp.inf, work)
        vals.append(mx)
        idxs.append(ix)
        hots.append(hot)
    exps = [jnp.exp(v - vals[0]) for v in vals]
    denom = exps[0] + exps[1] + exps[2] + exps[3]
    sel = (hots[0] | hots[1] | hots[2] | hots[3]).astype(F32)
    prefix = _dot(sel.astype(BF16), tri_ref[...]) + carry_s[...]
    tm = lt.shape[1]
    pad = jnp.zeros((8 - TOP_K, tm), F32)
    topi_ref[...] = jnp.concatenate(idxs + [pad], axis=0).astype(jnp.int32)
    rank_ref[...] = jnp.concatenate(
        [jnp.sum(jnp.where(h, prefix, 0.0), axis=0, keepdims=True) for h in hots] + [pad],
        axis=0).astype(jnp.int32)
    gates_t = jnp.concatenate([e / denom for e in exps] + [jnp.zeros((LANES - TOP_K, tm), F32)], axis=0)
    gate_ref[...] = gates_t.T
    carry_s[...] = carry_s[...] + jnp.sum(sel, axis=1, keepdims=True)
    cnt_ref[...] = jnp.broadcast_to(carry_s[...], cnt_ref.shape)


def post_mixer(x, tile_mod, attn, pool, four, wo_b, mods, n2, rw_pad, rb_pad, tri):
    n, d = x.shape
    tm = TOKEN_TILE
    tok = lambda i, tmod: (i, 0)
    const = lambda i, tmod: (0, 0)
    lane_tok = lambda i, tmod: (0, i)
    return pl.pallas_call(
        _post_mixer_kernel,
        out_shape=(jax.ShapeDtypeStruct((n, d), F32),
                   jax.ShapeDtypeStruct((n, d), BF16),
                   jax.ShapeDtypeStruct((8, n), jnp.int32),
                   jax.ShapeDtypeStruct((8, n), jnp.int32),
                   jax.ShapeDtypeStruct((n, LANES), F32),
                   jax.ShapeDtypeStruct((LANES, LANES), F32)),
        grid_spec=pltpu.PrefetchScalarGridSpec(
            num_scalar_prefetch=1,
            grid=(n // tm,),
            in_specs=[
                pl.BlockSpec((tm, d), tok),
                pl.BlockSpec((tm, ATTN_W), tok),
                pl.BlockSpec((tm, POOL_W), tok),
                pl.BlockSpec((tm, FOUR_W), tok),
                pl.BlockSpec((d, d), const),
                pl.BlockSpec((1, 6, 1, d), lambda i, tmod: (tmod[i], 0, 0, 0)),
                pl.BlockSpec((1, d), const),
                pl.BlockSpec((d, LANES), const),
                pl.BlockSpec((1, LANES), const),
                pl.BlockSpec((tm, tm), const),
            ],
            out_specs=(pl.BlockSpec((tm, d), tok),
                       pl.BlockSpec((tm, d), tok),
                       pl.BlockSpec((8, tm), lane_tok),
                       pl.BlockSpec((8, tm), lane_tok),
                       pl.BlockSpec((tm, LANES), tok),
                       pl.BlockSpec((LANES, LANES), const)),
            scratch_shapes=[pltpu.VMEM((LANES, 1), F32)],
        ),
        compiler_params=_cparams(("arbitrary",)),
        name="post_mixer_router",
    )(tile_mod, x, attn, pool, four, wo_b, mods, n2.reshape(1, d), rw_pad, rb_pad, tri)


def _expert_kernel(be_ref, nused_ref, x_ref, wgu_ref, bgu_ref, wd_ref, bd_ref, y_ref):
    b = pl.program_id(0)

    @pl.when(b < nused_ref[0])
    def _():
        gu = _dot(x_ref[...], wgu_ref[0]) + bgu_ref[0]
        f = gu.shape[1] // 2
        glu = jnp.minimum(gu[:, :f], SWIGLU_LIMIT)
        lin = jnp.clip(gu[:, f:], -SWIGLU_LIMIT, SWIGLU_LIMIT)
        act = glu * jax.nn.sigmoid(SWIGLU_ALPHA * glu) * (lin + 1.0)
        y_ref[...] = _dot(act.astype(BF16), wd_ref[0]) + bd_ref[0]

    @pl.when(b >= nused_ref[0])
    def _():
        y_ref[...] = jnp.zeros(y_ref.shape, y_ref.dtype)


def expert_ffn(block_expert, n_used, x_rows, wgu_b, bgu, wd_b, bd):
    n_rows, d = x_rows.shape
    e, _, f2 = wgu_b.shape
    bm = EXPERT_BLOCK
    return pl.pallas_call(
        _expert_kernel,
        out_shape=jax.ShapeDtypeStruct((n_rows, d), F32),
        grid_spec=pltpu.PrefetchScalarGridSpec(
            num_scalar_prefetch=2,
            grid=(n_rows // bm,),
            in_specs=[
                pl.BlockSpec((bm, d), lambda b, be, nu: (b, 0)),
                pl.BlockSpec((1, d, f2), lambda b, be, nu: (be[b], 0, 0)),
                pl.BlockSpec((1, 1, f2), lambda b, be, nu: (be[b], 0, 0)),
                pl.BlockSpec((1, f2 // 2, d), lambda b, be, nu: (be[b], 0, 0)),
                pl.BlockSpec((1, 1, d), lambda b, be, nu: (be[b], 0, 0)),
            ],
            out_specs=pl.BlockSpec((bm, d), lambda b, be, nu: (b, 0)),
        ),
        compiler_params=_cparams(("parallel",)),
        name="expert_ffn",
    )(block_expert, n_used, x_rows, wgu_b, bgu.reshape(e, 1, f2), wd_b, bd.reshape(e, 1, d))


def _combine_kernel(tmod_ref, x_ref, y_ref, g_ref, mod_ref, o_ref):
    del tmod_ref
    g = g_ref[...]
    acc = g[:, 0:1] * y_ref[0]
    for k in range(1, TOP_K):
        acc = acc + g[:, k:k + 1] * y_ref[k]
    o_ref[...] = x_ref[...] + mod_ref[0, 5] * acc


def moe_combine(x1, tile_mod, y4, gates, mods):
    n, d = x1.shape
    tm = TOKEN_TILE
    return pl.pallas_call(
        _combine_kernel,
        out_shape=jax.ShapeDtypeStruct((n, d), F32),
        grid_spec=pltpu.PrefetchScalarGridSpec(
            num_scalar_prefetch=1,
            grid=(n // tm,),
            in_specs=[
                pl.BlockSpec((tm, d), lambda i, tmod: (i, 0)),
                pl.BlockSpec((TOP_K, tm, d), lambda i, tmod: (0, i, 0)),
                pl.BlockSpec((tm, LANES), lambda i, tmod: (i, 0)),
                pl.BlockSpec((1, 6, 1, d), lambda i, tmod: (tmod[i], 0, 0, 0)),
            ],
            out_specs=pl.BlockSpec((tm, d), lambda i, tmod: (i, 0)),
        ),
        compiler_params=_cparams(("parallel",)),
        name="moe_combine",
    )(tile_mod, x1, y4, gates, mods)


def _rope_tables(t):
    quarter = HEAD_DIM // 4
    lane = jnp.arange(ATTN_W)
    inv_freq = ROPE_THETA ** (-(lane % quarter).astype(F32) / quarter)
    pos_t = jnp.arange(t)
    pos = jnp.where((lane % HEAD_DIM)[None, :] < HEAD_DIM // 2,
                    (pos_t // GRID_W)[:, None], (pos_t % GRID_W)[:, None]).astype(F32)
    ang = pos * inv_freq[None, :]
    sign = jnp.where(lane % (2 * quarter) < quarter, -1.0, 1.0).astype(F32)
    return jnp.cos(ang), jnp.sin(ang) * sign[None, :]


def _dft_tables(t):
    k = jnp.arange(t, dtype=jnp.int32)
    r = (k[:, None] * k[None, :]) % t
    ang = r.astype(F32) * (2.0 * math.pi / t)
    scale = 1.0 / math.sqrt(t)
    return (jnp.cos(ang) * scale).astype(BF16), (jnp.sin(ang) * scale).astype(BF16)


def _block_diag(blocks):
    g, c, d = blocks.shape
    eye = jnp.eye(g, dtype=blocks.dtype)
    return (eye[:, None, :, None] * blocks[:, :, None, :]).reshape(g * c, g * d)


def _fourier_weights(fourier_w_l):
    c = jnp.arange(FOUR_GROUP, dtype=jnp.int32)
    ang = ((c[:, None] * c[None, :]) % FOUR_GROUP).astype(F32) * (2.0 * math.pi / FOUR_GROUP)
    scale = 1.0 / math.sqrt(FOUR_GROUP)
    hp = lax.Precision.HIGHEST
    a = jnp.einsum("ck,gkd->gcd", jnp.cos(ang) * scale, fourier_w_l, precision=hp)
    b = jnp.einsum("ck,gkd->gcd", jnp.sin(ang) * scale, fourier_w_l, precision=hp)
    return jnp.concatenate([_block_diag(a), _block_diag(b)], axis=1)


def kernel(x_prompt, x_sample, c, c_ctx, cache_k, cache_v, ada_w, ada_b, norm1_g, norm2_g, w_in,
           q_norm_g, k_norm_g, lambda_qk, subln_g, pool_w, pool_scale, fourier_w, w_out,
           router_w, router_b, w_gate_up, b_gate_up, w_down, b_down):
    bp, tp, d = x_prompt.shape
    bs, ts, _ = x_sample.shape
    depth = ada_w.shape[0]
    past = cache_k.shape[2]
    tm = TOKEN_TILE
    n_p, n_s = bp * tp, bs * ts
    n = n_p + n_s
    assert tp % tm == 0 and ts % tm == 0 and past % KV_CHUNK == 0 and bs + 1 <= MOD_ROWS

    tile_mod = np.concatenate([np.zeros(n_p // tm, np.int32),
                               1 + np.repeat(np.arange(bs, dtype=np.int32), ts // tm)])
    tile_t0 = np.concatenate([np.tile(np.arange(0, tp, tm, dtype=np.int32), bp),
                              np.tile(np.arange(0, ts, tm, dtype=np.int32), bs)])
    tile_len = np.concatenate([np.full(n_p // tm, tp, np.int32), np.full(n_s // tm, ts, np.int32)])
    tile_mod, tile_t0, tile_len = map(jnp.asarray, (tile_mod, tile_t0, tile_len))

    cond = jnp.zeros((MOD_ROWS, d), F32).at[0].set(c_ctx).at[1:1 + bs].set(c)
    mods_all = modulation_all(cond, ada_w, ada_b).reshape(depth, MOD_ROWS, 6, 1, d)

    cos_t, sin_t = _rope_tables(ts)
    dft_p = _dft_tables(tp)
    dft_s = _dft_tables(ts)
    gmat = _block_diag(jnp.ones((ATTN_W // HEAD_DIM, HEAD_DIM, HEAD_DIM), BF16))
    tri = (jnp.arange(tm)[:, None] < jnp.arange(tm)[None, :]).astype(BF16)
    tile_w = ATTN_W // HEAD_DIM

    n_assign = n * TOP_K
    bm = EXPERT_BLOCK
    n_blocks = -(-n_assign // bm) + N_EXPERTS
    n_rows = n_blocks * bm
    tok_ids = jnp.tile(jnp.arange(n, dtype=jnp.int32), TOP_K)

    x = jnp.concatenate([x_prompt.reshape(n_p, d), x_sample.reshape(n_s, d)], axis=0)
    new_k, new_v = [], []
    for l in range(depth):
        lam_init = 0.8 - 0.6 * math.exp(-0.3 * l)
        mods = mods_all[l]
        lf = lambda_qk[l]
        lam = (jnp.exp(jnp.sum(lf[0] * lf[1])) - jnp.exp(jnp.sum(lf[2] * lf[3])) + lam_init).reshape(1)

        ab = _fourier_weights(fourier_w[l]).astype(BF16)
        proj, za, zb = pre_mixer(x, tile_mod, norm1_g[l], mods, w_in[l].astype(BF16), ab)

        qg = jnp.tile(q_norm_g[l], tile_w).reshape(1, ATTN_W)
        kg = jnp.tile(k_norm_g[l], tile_w).reshape(1, ATTN_W)

        qt_p, kk_p, vt_p, kn_p = qkv_prep(proj, 0, bp, tp, gmat, qg, kg, None, None,
                                          rope=False, emit_kn=True)
        attn_p = diff_attention(lam, qt_p, kk_p, vt_p, subln_g[l], lam_init)
        new_k.append(kn_p.reshape(bp, tp, N_HEADS, 2, HEAD_DIM))
        new_v.append(proj[:n_p, 2 * ATTN_W:3 * ATTN_W].reshape(bp, tp, N_HEADS, V_DIM))

        qt_s, kk_s, vt_s = qkv_prep(proj, n_p, bs, ts, gmat, qg, kg, cos_t, sin_t,
                                    rope=True, emit_kn=False)
        ck = cache_k[:, l].reshape(bs, past, ATTN_W).astype(BF16)
        cv = cache_v[:, l].reshape(bs, past // KV_CHUNK, KV_CHUNK, ATTN_W).astype(BF16)
        kk_all = jnp.concatenate([kk_s, ck], axis=1)
        vt_all = jnp.concatenate([vt_s, cv.transpose(0, 1, 3, 2)], axis=1)
        attn_s = diff_attention(lam, qt_s, kk_all, vt_all, subln_g[l], lam_init)
        attn = jnp.concatenate([attn_p, attn_s], axis=0)

        pool = multiscale_pool(proj, tile_t0, tile_len, _block_diag(pool_w[l]).astype(BF16),
                               pool_scale[l])

        four_p = fourier_mix(dft_p[0], dft_p[1],
                             za[:n_p].reshape(bp, tp, FOUR_W), zb[:n_p].reshape(bp, tp, FOUR_W))
        four_s = fourier_mix(dft_s[0], dft_s[1],
                             za[n_p:].reshape(bs, ts, FOUR_W), zb[n_p:].reshape(bs, ts, FOUR_W))
        four = jnp.concatenate([four_p.reshape(n_p, FOUR_W), four_s.reshape(n_s, FOUR_W)], axis=0)

        rw_pad = jnp.zeros((d, LANES), F32).at[:, :N_EXPERTS].set(router_w[l])
        rb_pad = jnp.full((1, LANES), -1e30, F32).at[0, :N_EXPERTS].set(router_b[l])
        x1, h2, top_i, rank, gates, counts = post_mixer(
            x, tile_mod, attn, pool, four, w_out[l].astype(BF16), mods, norm2_g[l], rw_pad, rb_pad, tri)

        cnt = counts[:N_EXPERTS, 0].astype(jnp.int32)
        padded = (cnt + bm - 1) // bm * bm
        pad_end = jnp.cumsum(padded)
        pad_start = pad_end - padded
        dest = (pad_start[top_i[:TOP_K]] + rank[:TOP_K]).reshape(-1)
        block_expert = jnp.minimum(
            jnp.searchsorted(pad_end, jnp.arange(n_blocks, dtype=jnp.int32) * bm, side="right"),
            N_EXPERTS - 1).astype(jnp.int32)
        n_used = (pad_end[-1] // bm).astype(jnp.int32).reshape(1)
        row_token = jnp.zeros((n_rows,), jnp.int32).at[dest].set(tok_ids)

        x_rows = jnp.take(h2, row_token, axis=0)
        y_rows = expert_ffn(block_expert, n_used, x_rows, w_gate_up[l].astype(BF16), b_gate_up[l],
                            w_down[l].astype(BF16), b_down[l])
        y4 = jnp.take(y_rows, dest, axis=0).reshape(TOP_K, n, d)
        x = moe_combine(x1, tile_mod, y4, gates, mods)

    y_prompt = x[:n_p].reshape(bp, tp, d)
    y_sample = x[n_p:].reshape(bs, ts, d)
    return y_prompt, y_sample, jnp.stack(new_k, axis=1), jnp.stack(new_v, axis=1)
```

```python
import functools
import math

import numpy as np
import jax
import jax.numpy as jnp
from jax import lax
from jax.experimental import pallas as pl
from jax.experimental.pallas import tpu as pltpu

F32 = jnp.float32
BF16 = jnp.bfloat16

GRID_W = 64
HEAD_DIM = 64
V_DIM = 2 * HEAD_DIM
N_HEADS = 4
ATTN_W = N_HEADS * V_DIM
POOL_WINDOWS = (2, 4, 8, 16)
POOL_GROUP = 64
POOL_W = 256
FOUR_GROUPS = 4
FOUR_GROUP = 64
FOUR_W = 256
N_EXPERTS = 32
TOP_K = 4
SWIGLU_LIMIT = 7.0
SWIGLU_ALPHA = 1.702
ROPE_THETA = 10000.0
RMS_EPS = 1e-6

LANES = 128
TOKEN_TILE = 256
POOL_HALO = 8
KV_CHUNK = 256
ATTN_STEP_CHUNKS = 2
EXPERT_BLOCK = 256
DFT_ROW_TILE = 512
MOD_ROWS = 16
VMEM_LIMIT = 56 * 1024 * 1024


def _dot(a, b):
    return jnp.dot(a, b, preferred_element_type=F32)


def _split_bf16(x):
    hi = x.astype(BF16)
    lo = (x - hi.astype(F32)).astype(BF16)
    return hi, lo


def _cparams(sem):
    return pltpu.CompilerParams(dimension_semantics=sem, vmem_limit_bytes=VMEM_LIMIT)


def _mod_kernel(cond_ref, w_ref, b_ref, o_ref):
    c = cond_ref[...]
    a = c * jax.nn.sigmoid(c)
    ah, al = _split_bf16(a)
    wh, wl = _split_bf16(w_ref[0])
    o_ref[0] = _dot(ah, wh) + _dot(ah, wl) + _dot(al, wh) + b_ref[0]


def modulation_all(cond, ada_w, ada_b):
    depth, d, d6 = ada_w.shape
    tn = d6 // 4
    return pl.pallas_call(
        _mod_kernel,
        out_shape=jax.ShapeDtypeStruct((depth, MOD_ROWS, d6), F32),
        grid=(depth, d6 // tn),
        in_specs=[
            pl.BlockSpec((MOD_ROWS, d), lambda l, j: (0, 0)),
            pl.BlockSpec((1, d, tn), lambda l, j: (l, 0, j)),
            pl.BlockSpec((1, 1, tn), lambda l, j: (l, 0, j)),
        ],
        out_specs=pl.BlockSpec((1, MOD_ROWS, tn), lambda l, j: (l, 0, j)),
        compiler_params=_cparams(("parallel", "parallel")),
        name="modulation",
    )(cond, ada_w, ada_b.reshape(depth, 1, d6))


def _rms_mod(x, g, scale, shift):
    ms = jnp.mean(x * x, axis=-1, keepdims=True)
    return x * lax.rsqrt(ms + RMS_EPS) * g * (1.0 + scale) + shift


def _pre_mixer_kernel(tmod_ref, x_ref, g_ref, mod_ref, w_ref, ab_ref, proj_ref, za_ref, zb_ref):
    del tmod_ref
    h = _rms_mod(x_ref[...], g_ref[...], mod_ref[0, 1], mod_ref[0, 0])
    proj = _dot(h.astype(BF16), w_ref[...])
    nproj = proj_ref.shape[1]
    proj_ref[...] = proj[:, :nproj]
    z = _dot(proj[:, nproj:].astype(BF16), ab_ref[...])
    za_ref[...] = z[:, :FOUR_W].astype(BF16)
    zb_ref[...] = z[:, FOUR_W:].astype(BF16)


def pre_mixer(x, tile_mod, n1, mods, w_in_b, ab_b):
    n, d = x.shape
    in_w = w_in_b.shape[1]
    nproj = in_w - FOUR_W
    tm = TOKEN_TILE
    return pl.pallas_call(
        _pre_mixer_kernel,
        out_shape=(jax.ShapeDtypeStruct((n, nproj), F32),
                   jax.ShapeDtypeStruct((n, FOUR_W), BF16),
                   jax.ShapeDtypeStruct((n, FOUR_W), BF16)),
        grid_spec=pltpu.PrefetchScalarGridSpec(
            num_scalar_prefetch=1,
            grid=(n // tm,),
            in_specs=[
                pl.BlockSpec((tm, d), lambda i, tmod: (i, 0)),
                pl.BlockSpec((1, d), lambda i, tmod: (0, 0)),
                pl.BlockSpec((1, 6, 1, d), lambda i, tmod: (tmod[i], 0, 0, 0)),
                pl.BlockSpec((d, in_w), lambda i, tmod: (0, 0)),
                pl.BlockSpec((FOUR_W, 2 * FOUR_W), lambda i, tmod: (0, 0)),
            ],
            out_specs=(pl.BlockSpec((tm, nproj), lambda i, tmod: (i, 0)),
                       pl.BlockSpec((tm, FOUR_W), lambda i, tmod: (i, 0)),
                       pl.BlockSpec((tm, FOUR_W), lambda i, tmod: (i, 0))),
        ),
        compiler_params=_cparams(("parallel",)),
        name="pre_mixer",
    )(tile_mod, x, n1.reshape(1, d), mods, w_in_b, ab_b)


def _group_rms(x, gmat, g):
    sh, sl = _split_bf16(x * x)
    ss = _dot(sh, gmat) + _dot(sl, gmat)
    return x * lax.rsqrt(ss * (1.0 / HEAD_DIM) + RMS_EPS) * g


def _rope(y, cos, sin_signed):
    w = y.shape[1]
    q = HEAD_DIM // 4
    lane = lax.broadcasted_iota(jnp.int32, y.shape, 1)
    partner = jnp.where(lane % (2 * q) < q, pltpu.roll(y, w - q, axis=1), pltpu.roll(y, q, axis=1))
    return y * cos + partner * sin_signed


def _qkv_kernel(q_ref, k_ref, v_ref, gmat_ref, qg_ref, kg_ref, *rest, rope, emit_kn):
    if rope:
        cos_ref, sin_ref, *outs = rest
    else:
        outs = rest
    qt_ref, kk_ref, vt_ref = outs[:3]
    gmat = gmat_ref[...]
    qn = _group_rms(q_ref[...], gmat, qg_ref[...])
    kn = _group_rms(k_ref[...], gmat, kg_ref[...])
    if emit_kn:
        outs[3][0] = kn
    if rope:
        cos = cos_ref[...]
        sin = sin_ref[...]
        qn = _rope(qn, cos, sin)
        kn = _rope(kn, cos, sin)
    qt_ref[0] = (qn * (HEAD_DIM ** -0.5 * math.log2(math.e))).T.astype(BF16)
    kk_ref[0] = kn.astype(BF16)
    vt_ref[0, 0] = v_ref[...].T.astype(BF16)


def qkv_prep(proj, row_off, b, t, gmat, qg, kg, cos, sin, *, rope, emit_kn):
    tm = TOKEN_TILE
    nt = t // tm
    off = row_off // tm
    w = ATTN_W
    row = lambda bi, j: (off + bi * nt + j)
    in_specs = [
        pl.BlockSpec((tm, w), lambda bi, j: (row(bi, j), 0)),
        pl.BlockSpec((tm, w), lambda bi, j: (row(bi, j), 1)),
        pl.BlockSpec((tm, w), lambda bi, j: (row(bi, j), 2)),
        pl.BlockSpec((w, w), lambda bi, j: (0, 0)),
        pl.BlockSpec((1, w), lambda bi, j: (0, 0)),
        pl.BlockSpec((1, w), lambda bi, j: (0, 0)),
    ]
    args = [proj, proj, proj, gmat, qg, kg]
    if rope:
        in_specs += [pl.BlockSpec((tm, w), lambda bi, j: (j, 0)),
                     pl.BlockSpec((tm, w), lambda bi, j: (j, 0))]
        args += [cos, sin]
    out_shape = [jax.ShapeDtypeStruct((b, w, t), BF16),
                 jax.ShapeDtypeStruct((b, t, w), BF16),
                 jax.ShapeDtypeStruct((b, nt, w, tm), BF16)]
    out_specs = [pl.BlockSpec((1, w, tm), lambda bi, j: (bi, 0, j)),
                 pl.BlockSpec((1, tm, w), lambda bi, j: (bi, j, 0)),
                 pl.BlockSpec((1, 1, w, tm), lambda bi, j: (bi, j, 0, 0))]
    if emit_kn:
        out_shape.append(jax.ShapeDtypeStruct((b, t, w), F32))
        out_specs.append(pl.BlockSpec((1, tm, w), lambda bi, j: (bi, j, 0)))
    return pl.pallas_call(
        functools.partial(_qkv_kernel, rope=rope, emit_kn=emit_kn),
        out_shape=tuple(out_shape),
        grid=(b, nt),
        in_specs=in_specs,
        out_specs=tuple(out_specs),
        compiler_params=_cparams(("parallel", "parallel")),
        name="qkv_prep_rope" if rope else "qkv_prep",
    )(*args)


def _attn_kernel(lam_ref, qt_ref, k_ref, vt_ref, g_ref, o_ref, *, nk, sub, lam_init):
    q = qt_ref[0]
    row = lax.broadcasted_iota(jnp.int32, q.shape, 0)
    zero = jnp.zeros_like(q)
    qmaps = (jnp.where(row < HEAD_DIM, q, zero), jnp.where(row >= HEAD_DIM, q, zero))
    tk = sub * KV_CHUNK

    def scores(c):
        k = k_ref[0, c * tk:(c + 1) * tk, :]
        return [_dot(k, qm) for qm in qmaps]

    m_run, l_run, acc = [None, None], [None, None], [None, None]
    s_next = scores(0)
    for c in range(nk):
        s_cur = s_next
        if c + 1 < nk:
            s_next = scores(c + 1)
        for m in range(2):
            s = s_cur[m]
            m_blk = jnp.max(s, axis=0, keepdims=True)
            m_new = m_blk if c == 0 else jnp.maximum(m_run[m], m_blk)
            p = jnp.exp2(s - m_new)
            pb = p.astype(BF16)
            pv = _dot(vt_ref[0, c * sub], pb[:KV_CHUNK])
            for j in range(1, sub):
                pv = pv + _dot(vt_ref[0, c * sub + j], pb[j * KV_CHUNK:(j + 1) * KV_CHUNK])
            l_blk = jnp.sum(p, axis=0, keepdims=True)
            if c == 0:
                l_run[m], acc[m] = l_blk, pv
            else:
                alpha = jnp.exp2(m_run[m] - m_new)
                l_run[m] = alpha * l_run[m] + l_blk
                acc[m] = alpha * acc[m] + pv
            m_run[m] = m_new

    o = acc[0] / l_run[0] - lam_ref[0] * (acc[1] / l_run[1])
    ms = jnp.mean(o * o, axis=0, keepdims=True)
    o = o * lax.rsqrt(ms + RMS_EPS) * g_ref[...] * (1.0 - lam_init)
    o_ref[...] = o.T.astype(BF16)


def diff_attention(lam, qt, kk, vt, subln_g, lam_init):
    b, w, t = qt.shape
    tk = kk.shape[1]
    nk = tk // KV_CHUNK
    sub = ATTN_STEP_CHUNKS if nk % ATTN_STEP_CHUNKS == 0 else 1
    tq = TOKEN_TILE
    nq = t // tq
    hw = V_DIM
    return pl.pallas_call(
        functools.partial(_attn_kernel, nk=nk // sub, sub=sub, lam_init=lam_init),
        out_shape=jax.ShapeDtypeStruct((b * t, w), BF16),
        grid_spec=pltpu.PrefetchScalarGridSpec(
            num_scalar_prefetch=1,
            grid=(b, N_HEADS, nq),
            in_specs=[
                pl.BlockSpec((1, hw, tq), lambda bi, h, qi, lam: (bi, h, qi)),
                pl.BlockSpec((1, tk, hw), lambda bi, h, qi, lam: (bi, 0, h)),
                pl.BlockSpec((1, nk, hw, KV_CHUNK), lambda bi, h, qi, lam: (bi, 0, h, 0)),
                pl.BlockSpec((hw, 1), lambda bi, h, qi, lam: (0, 0)),
            ],
            out_specs=pl.BlockSpec((tq, hw), lambda bi, h, qi, lam: (bi * nq + qi, h)),
        ),
        compiler_params=_cparams(("parallel", "parallel", "parallel")),
        name="diff_attention",
    )(lam, qt, kk, vt, subln_g.reshape(hw, 1))


def _pool_kernel(t0_ref, tlen_ref, prev_ref, cur_ref, next_ref, pw_ref, ps_ref, o_ref):
    i = pl.program_id(0)
    t0 = t0_ref[i]
    tlen = tlen_ref[i]
    tm = cur_ref.shape[0]
    u = cur_ref[...]
    prev = jnp.where(t0 > 0, prev_ref[...], 0.0)
    nxt = jnp.where(t0 + tm < tlen, next_ref[...], 0.0)
    e = jnp.concatenate([prev, u, nxt], axis=0)
    n = e.shape[0]
    w2 = e + pltpu.roll(e, 1, axis=0)
    w4 = pltpu.roll(w2, 1, axis=0) + pltpu.roll(w2, n - 1, axis=0)
    w8 = pltpu.roll(w4, 2, axis=0) + pltpu.roll(w4, n - 2, axis=0)
    w16 = pltpu.roll(w8, 4, axis=0) + pltpu.roll(w8, n - 4, axis=0)
    sums = [w[POOL_HALO:POOL_HALO + tm] for w in (w2, w4, w8, w16)]
    lane = lax.broadcasted_iota(jnp.int32, u.shape, 1)
    grp = lane // POOL_GROUP
    tabs = t0 + lax.broadcasted_iota(jnp.int32, u.shape, 0)
    total = sums[3]
    half = jnp.full(u.shape, POOL_WINDOWS[3] // 2, jnp.int32)
    for gi in (2, 1, 0):
        total = jnp.where(grp == gi, sums[gi], total)
        half = jnp.where(grp == gi, POOL_WINDOWS[gi] // 2, half)
    cnt = jnp.minimum(tabs + half, tlen) - jnp.maximum(tabs - half, 0)
    d = total / cnt.astype(F32) - u
    o_ref[...] = (_dot(d.astype(BF16), pw_ref[...]) * ps_ref[...]).astype(BF16)


def multiscale_pool(proj, tile_t0, tile_len, pw_b, pool_scale):
    n = proj.shape[0]
    tm = TOKEN_TILE
    hb = tm // POOL_HALO
    col = (3 * ATTN_W) // POOL_W
    last = n // POOL_HALO - 1
    return pl.pallas_call(
        _pool_kernel,
        out_shape=jax.ShapeDtypeStruct((n, POOL_W), BF16),
        grid_spec=pltpu.PrefetchScalarGridSpec(
            num_scalar_prefetch=2,
            grid=(n // tm,),
            in_specs=[
                pl.BlockSpec((POOL_HALO, POOL_W), lambda i, a, b: (jnp.maximum(i * hb - 1, 0), col)),
                pl.BlockSpec((tm, POOL_W), lambda i, a, b: (i, col)),
                pl.BlockSpec((POOL_HALO, POOL_W), lambda i, a, b: (jnp.minimum((i + 1) * hb, last), col)),
                pl.BlockSpec((POOL_W, POOL_W), lambda i, a, b: (0, 0)),
                pl.BlockSpec((1, POOL_W), lambda i, a, b: (0, 0)),
            ],
            out_specs=pl.BlockSpec((tm, POOL_W), lambda i, a, b: (i, 0)),
        ),
        compiler_params=_cparams(("parallel",)),
        name="multiscale_pool",
    )(tile_t0, tile_len, proj, proj, proj, pw_b, pool_scale.reshape(1, POOL_W))


def _fourier_kernel(c_ref, s_ref, za_ref, zb_ref, o_ref):
    o_ref[0] = (_dot(c_ref[...], za_ref[0]) - _dot(s_ref[...], zb_ref[0])).astype(BF16)


def fourier_mix(cmat, smat, za, zb):
    b, t, w = za.shape
    tr = min(DFT_ROW_TILE, t)
    return pl.pallas_call(
        _fourier_kernel,
        out_shape=jax.ShapeDtypeStruct((b, t, w), BF16),
        grid=(t // tr, b),
        in_specs=[
            pl.BlockSpec((tr, t), lambda r, bi: (r, 0)),
            pl.BlockSpec((tr, t), lambda r, bi: (r, 0)),
            pl.BlockSpec((1, t, w), lambda r, bi: (bi, 0, 0)),
            pl.BlockSpec((1, t, w), lambda r, bi: (bi, 0, 0)),
        ],
        out_specs=pl.BlockSpec((1, tr, w), lambda r, bi: (bi, r, 0)),
        compiler_params=_cparams(("parallel", "parallel")),
        name="fourier_mix",
    )(cmat, smat, za, zb)


def _post_mixer_kernel(tmod_ref, x_ref, a_ref, p_ref, f_ref, wo_ref, mod_ref, g2_ref, rw_ref, rb_ref,
                       tri_ref, x1_ref, h2_ref, topi_ref, rank_ref, gate_ref, cnt_ref, carry_s):
    del tmod_ref
    i = pl.program_id(0)

    @pl.when(i == 0)
    def _():
        carry_s[...] = jnp.zeros(carry_s.shape, F32)

    mixed = jnp.concatenate([a_ref[...], p_ref[...], f_ref[...]], axis=1)
    x1 = x_ref[...] + mod_ref[0, 2] * _dot(mixed, wo_ref[...])
    x1_ref[...] = x1
    h2 = _rms_mod(x1, g2_ref[...], mod_ref[0, 4], mod_ref[0, 3])
    hh, hl = _split_bf16(h2)
    h2_ref[...] = hh

    rw = rw_ref[...]
    rwh, rwl = _split_bf16(rw)
    logits = _dot(hh, rwh) + _dot(hh, rwl) + _dot(hl, rwh) + rb_ref[...]
    lt = logits.T
    eidx = lax.broadcasted_iota(jnp.int32, lt.shape, 0).astype(F32)
    work = lt
    vals, idxs, hots = [], [], []
    for _ in range(TOP_K):
        mx = jnp.max(work, axis=0, keepdims=True)
        ix = jnp.min(jnp.where(work == mx, eidx, float(LANES)), axis=0, keepdims=True)
        hot = eidx == ix
        work = jnp.where(hot, -jnp.inf, work)
        vals.append(mx)
        idxs.append(ix)
        hots.append(hot)
    exps = [jnp.exp(v - vals[0]) for v in vals]
    denom = exps[0] + exps[1] + exps[2] + exps[3]
    sel = (hots[0] | hots[1] | hots[2] | hots[3]).astype(F32)
    prefix = _dot(sel.astype(BF16), tri_ref[...]) + carry_s[...]
    tm = lt.shape[1]
    pad = jnp.zeros((8 - TOP_K, tm), F32)
    topi_ref[...] = jnp.concatenate(idxs + [pad], axis=0).astype(jnp.int32)
    rank_ref[...] = jnp.concatenate(
        [jnp.sum(jnp.where(h, prefix, 0.0), axis=0, keepdims=True) for h in hots] + [pad],
        axis=0).astype(jnp.int32)
    gates_t = jnp.concatenate([e / denom for e in exps] + [jnp.zeros((LANES - TOP_K, tm), F32)], axis=0)
    gate_ref[...] = gates_t.T
    carry_s[...] = carry_s[...] + jnp.sum(sel, axis=1, keepdims=True)
    cnt_ref[...] = jnp.broadcast_to(carry_s[...], cnt_ref.shape)


def post_mixer(x, tile_mod, attn, pool, four, wo_b, mods, n2, rw_pad, rb_pad, tri):
    n, d = x.shape
    tm = TOKEN_TILE
    tok = lambda i, tmod: (i, 0)
    const = lambda i, tmod: (0, 0)
    lane_tok = lambda i, tmod: (0, i)
    return pl.pallas_call(
        _post_mixer_kernel,
        out_shape=(jax.ShapeDtypeStruct((n, d), F32),
                   jax.ShapeDtypeStruct((n, d), BF16),
                   jax.ShapeDtypeStruct((8, n), jnp.int32),
                   jax.ShapeDtypeStruct((8, n), jnp.int32),
                   jax.ShapeDtypeStruct((n, LANES), F32),
                   jax.ShapeDtypeStruct((LANES, LANES), F32)),
        grid_spec=pltpu.PrefetchScalarGridSpec(
            num_scalar_prefetch=1,
            grid=(n // tm,),
            in_specs=[
                pl.BlockSpec((tm, d), tok),
                pl.BlockSpec((tm, ATTN_W), tok),
                pl.BlockSpec((tm, POOL_W), tok),
                pl.BlockSpec((tm, FOUR_W), tok),
                pl.BlockSpec((d, d), const),
                pl.BlockSpec((1, 6, 1, d), lambda i, tmod: (tmod[i], 0, 0, 0)),
                pl.BlockSpec((1, d), const),
                pl.BlockSpec((d, LANES), const),
                pl.BlockSpec((1, LANES), const),
                pl.BlockSpec((tm, tm), const),
            ],
            out_specs=(pl.BlockSpec((tm, d), tok),
                       pl.BlockSpec((tm, d), tok),
                       pl.BlockSpec((8, tm), lane_tok),
                       pl.BlockSpec((8, tm), lane_tok),
                       pl.BlockSpec((tm, LANES), tok),
                       pl.BlockSpec((LANES, LANES), const)),
            scratch_shapes=[pltpu.VMEM((LANES, 1), F32)],
        ),
        compiler_params=_cparams(("arbitrary",)),
        name="post_mixer_router",
    )(tile_mod, x, attn, pool, four, wo_b, mods, n2.reshape(1, d), rw_pad, rb_pad, tri)


def _expert_kernel(be_ref, nused_ref, x_ref, wgu_ref, bgu_ref, wd_ref, bd_ref, y_ref, wgu_s, wd_s):
    b = pl.program_id(0)

    @pl.when((b == 0) | (be_ref[b] != be_ref[jnp.maximum(b - 1, 0)]))
    def _():
        wgu_s[...] = wgu_ref[0, 0].astype(BF16)
        wd_s[...] = wd_ref[0, 0].astype(BF16)

    @pl.when(b < nused_ref[0])
    def _():
        gu = _dot(x_ref[...], wgu_s[...]) + bgu_ref[0]
        f = gu.shape[1] // 2
        glu = jnp.minimum(gu[:, :f], SWIGLU_LIMIT)
        lin = jnp.clip(gu[:, f:], -SWIGLU_LIMIT, SWIGLU_LIMIT)
        act = glu * jax.nn.sigmoid(SWIGLU_ALPHA * glu) * (lin + 1.0)
        y_ref[...] = _dot(act.astype(BF16), wd_s[...]) + bd_ref[0]

    @pl.when(b >= nused_ref[0])
    def _():
        y_ref[...] = jnp.zeros(y_ref.shape, y_ref.dtype)


def expert_ffn(block_expert, n_used, x_rows, layer, wgu, bgu, wd, bd):
    n_rows, d = x_rows.shape
    depth, e, _, f2 = wgu.shape
    bm = EXPERT_BLOCK
    return pl.pallas_call(
        _expert_kernel,
        out_shape=jax.ShapeDtypeStruct((n_rows, d), F32),
        grid_spec=pltpu.PrefetchScalarGridSpec(
            num_scalar_prefetch=2,
            grid=(n_rows // bm,),
            in_specs=[
                pl.BlockSpec((bm, d), lambda b, be, nu: (b, 0)),
                pl.BlockSpec((1, 1, d, f2), lambda b, be, nu: (layer, be[b], 0, 0)),
                pl.BlockSpec((1, 1, f2), lambda b, be, nu: (layer * e + be[b], 0, 0)),
                pl.BlockSpec((1, 1, f2 // 2, d), lambda b, be, nu: (layer, be[b], 0, 0)),
                pl.BlockSpec((1, 1, d), lambda b, be, nu: (layer * e + be[b], 0, 0)),
            ],
            out_specs=pl.BlockSpec((bm, d), lambda b, be, nu: (b, 0)),
            scratch_shapes=[pltpu.VMEM((d, f2), BF16), pltpu.VMEM((f2 // 2, d), BF16)],
        ),
        compiler_params=_cparams(("arbitrary",)),
        name="expert_ffn",
    )(block_expert, n_used, x_rows, wgu, bgu.reshape(depth * e, 1, f2), wd, bd.reshape(depth * e, 1, d))


def _combine_kernel(tmod_ref, x_ref, y_ref, g_ref, mod_ref, o_ref):
    del tmod_ref
    g = g_ref[...]
    acc = g[:, 0:1] * y_ref[0]
    for k in range(1, TOP_K):
        acc = acc + g[:, k:k + 1] * y_ref[k]
    o_ref[...] = x_ref[...] + mod_ref[0, 5] * acc


def moe_combine(x1, tile_mod, y4, gates, mods):
    n, d = x1.shape
    tm = TOKEN_TILE
    return pl.pallas_call(
        _combine_kernel,
        out_shape=jax.ShapeDtypeStruct((n, d), F32),
        grid_spec=pltpu.PrefetchScalarGridSpec(
            num_scalar_prefetch=1,
            grid=(n // tm,),
            in_specs=[
                pl.BlockSpec((tm, d), lambda i, tmod: (i, 0)),
                pl.BlockSpec((TOP_K, tm, d), lambda i, tmod: (0, i, 0)),
                pl.BlockSpec((tm, LANES), lambda i, tmod: (i, 0)),
                pl.BlockSpec((1, 6, 1, d), lambda i, tmod: (tmod[i], 0, 0, 0)),
            ],
            out_specs=pl.BlockSpec((tm, d), lambda i, tmod: (i, 0)),
        ),
        compiler_params=_cparams(("parallel",)),
        name="moe_combine",
    )(tile_mod, x1, y4, gates, mods)


def _rope_tables(t):
    quarter = HEAD_DIM // 4
    lane = jnp.arange(ATTN_W)
    inv_freq = ROPE_THETA ** (-(lane % quarter).astype(F32) / quarter)
    pos_t = jnp.arange(t)
    pos = jnp.where((lane % HEAD_DIM)[None, :] < HEAD_DIM // 2,
                    (pos_t // GRID_W)[:, None], (pos_t % GRID_W)[:, None]).astype(F32)
    ang = pos * inv_freq[None, :]
    sign = jnp.where(lane % (2 * quarter) < quarter, -1.0, 1.0).astype(F32)
    return jnp.cos(ang), jnp.sin(ang) * sign[None, :]


def _dft_tables(t):
    k = jnp.arange(t, dtype=jnp.int32)
    r = (k[:, None] * k[None, :]) % t
    ang = r.astype(F32) * (2.0 * math.pi / t)
    scale = 1.0 / math.sqrt(t)
    return (jnp.cos(ang) * scale).astype(BF16), (jnp.sin(ang) * scale).astype(BF16)


def _block_diag(blocks):
    g, c, d = blocks.shape
    eye = jnp.eye(g, dtype=blocks.dtype)
    return (eye[:, None, :, None] * blocks[:, :, None, :]).reshape(g * c, g * d)


def _fourier_weights(fourier_w_l):
    c = jnp.arange(FOUR_GROUP, dtype=jnp.int32)
    ang = ((c[:, None] * c[None, :]) % FOUR_GROUP).astype(F32) * (2.0 * math.pi / FOUR_GROUP)
    scale = 1.0 / math.sqrt(FOUR_GROUP)
    hp = lax.Precision.HIGHEST
    a = jnp.einsum("ck,gkd->gcd", jnp.cos(ang) * scale, fourier_w_l, precision=hp)
    b = jnp.einsum("ck,gkd->gcd", jnp.sin(ang) * scale, fourier_w_l, precision=hp)
    return jnp.concatenate([_block_diag(a), _block_diag(b)], axis=1)


def kernel(x_prompt, x_sample, c, c_ctx, cache_k, cache_v, ada_w, ada_b, norm1_g, norm2_g, w_in,
           q_norm_g, k_norm_g, lambda_qk, subln_g, pool_w, pool_scale, fourier_w, w_out,
           router_w, router_b, w_gate_up, b_gate_up, w_down, b_down):
    bp, tp, d = x_prompt.shape
    bs, ts, _ = x_sample.shape
    depth = ada_w.shape[0]
    past = cache_k.shape[2]
    tm = TOKEN_TILE
    n_p, n_s = bp * tp, bs * ts
    n = n_p + n_s
    assert tp % tm == 0 and ts % tm == 0 and past % KV_CHUNK == 0 and bs + 1 <= MOD_ROWS

    tile_mod = np.concatenate([np.zeros(n_p // tm, np.int32),
                               1 + np.repeat(np.arange(bs, dtype=np.int32), ts // tm)])
    tile_t0 = np.concatenate([np.tile(np.arange(0, tp, tm, dtype=np.int32), bp),
                              np.tile(np.arange(0, ts, tm, dtype=np.int32), bs)])
    tile_len = np.concatenate([np.full(n_p // tm, tp, np.int32), np.full(n_s // tm, ts, np.int32)])
    tile_mod, tile_t0, tile_len = map(jnp.asarray, (tile_mod, tile_t0, tile_len))

    cond = jnp.zeros((MOD_ROWS, d), F32).at[0].set(c_ctx).at[1:1 + bs].set(c)
    mods_all = modulation_all(cond, ada_w, ada_b).reshape(depth, MOD_ROWS, 6, 1, d)

    cos_t, sin_t = _rope_tables(ts)
    dft_p = _dft_tables(tp)
    dft_s = _dft_tables(ts)
    gmat = _block_diag(jnp.ones((ATTN_W // HEAD_DIM, HEAD_DIM, HEAD_DIM), BF16))
    tri = (jnp.arange(tm)[:, None] < jnp.arange(tm)[None, :]).astype(BF16)
    tile_w = ATTN_W // HEAD_DIM

    n_assign = n * TOP_K
    bm = EXPERT_BLOCK
    n_blocks = -(-n_assign // bm) + N_EXPERTS
    n_rows = n_blocks * bm
    tok_ids = jnp.tile(jnp.arange(n, dtype=jnp.int32), TOP_K)

    x = jnp.concatenate([x_prompt.reshape(n_p, d), x_sample.reshape(n_s, d)], axis=0)
    new_k, new_v = [], []
    for l in range(depth):
        lam_init = 0.8 - 0.6 * math.exp(-0.3 * l)
        mods = mods_all[l]
        lf = lambda_qk[l]
        lam = (jnp.exp(jnp.sum(lf[0] * lf[1])) - jnp.exp(jnp.sum(lf[2] * lf[3])) + lam_init).reshape(1)

        ab = _fourier_weights(fourier_w[l]).astype(BF16)
        proj, za, zb = pre_mixer(x, tile_mod, norm1_g[l], mods, w_in[l].astype(BF16), ab)

        qg = jnp.tile(q_norm_g[l], tile_w).reshape(1, ATTN_W)
        kg = jnp.tile(k_norm_g[l], tile_w).reshape(1, ATTN_W)

        qt_p, kk_p, vt_p, kn_p = qkv_prep(proj, 0, bp, tp, gmat, qg, kg, None, None,
                                          rope=False, emit_kn=True)
        attn_p = diff_attention(lam, qt_p, kk_p, vt_p, subln_g[l], lam_init)
        new_k.append(kn_p.reshape(bp, tp, N_HEADS, 2, HEAD_DIM))
        new_v.append(proj[:n_p, 2 * ATTN_W:3 * ATTN_W].reshape(bp, tp, N_HEADS, V_DIM))

        qt_s, kk_s, vt_s = qkv_prep(proj, n_p, bs, ts, gmat, qg, kg, cos_t, sin_t,
                                    rope=True, emit_kn=False)
        ck = cache_k[:, l].reshape(bs, past, ATTN_W).astype(BF16)
        cv = cache_v[:, l].reshape(bs, past // KV_CHUNK, KV_CHUNK, ATTN_W).astype(BF16)
        kk_all = jnp.concatenate([kk_s, ck], axis=1)
        vt_all = jnp.concatenate([vt_s, cv.transpose(0, 1, 3, 2)], axis=1)
        attn_s = diff_attention(lam, qt_s, kk_all, vt_all, subln_g[l], lam_init)
        attn = jnp.concatenate([attn_p, attn_s], axis=0)

        pool = multiscale_pool(proj, tile_t0, tile_len, _block_diag(pool_w[l]).astype(BF16),
                               pool_scale[l])

        four_p = fourier_mix(dft_p[0], dft_p[1],
                             za[:n_p].reshape(bp, tp, FOUR_W), zb[:n_p].reshape(bp, tp, FOUR_W))
        four_s = fourier_mix(dft_s[0], dft_s[1],
                             za[n_p:].reshape(bs, ts, FOUR_W), zb[n_p:].reshape(bs, ts, FOUR_W))
        four = jnp.concatenate([four_p.reshape(n_p, FOUR_W), four_s.reshape(n_s, FOUR_W)], axis=0)

        rw_pad = jnp.zeros((d, LANES), F32).at[:, :N_EXPERTS].set(router_w[l])
        rb_pad = jnp.full((1, LANES), -1e30, F32).at[0, :N_EXPERTS].set(router_b[l])
        x1, h2, top_i, rank, gates, counts = post_mixer(
            x, tile_mod, attn, pool, four, w_out[l].astype(BF16), mods, norm2_g[l], rw_pad, rb_pad, tri)

        cnt = counts[:N_EXPERTS, 0].astype(jnp.int32)
        padded = (cnt + bm - 1) // bm * bm
        pad_end = jnp.cumsum(padded)
        pad_start = pad_end - padded
        dest = (pad_start[top_i[:TOP_K]] + rank[:TOP_K]).reshape(-1)
        block_row0 = jnp.arange(n_blocks, dtype=jnp.int32) * bm
        block_expert = jnp.minimum(
            jnp.sum((pad_end[None, :] <= block_row0[:, None]).astype(jnp.int32), axis=1), N_EXPERTS - 1)
        n_used = (pad_end[-1] // bm).astype(jnp.int32).reshape(1)
        row_token = jnp.zeros((n_rows,), jnp.int32).at[dest].set(tok_ids)

        x_rows = jnp.take(h2, row_token, axis=0)
        y_rows = expert_ffn(block_expert, n_used, x_rows, l, w_gate_up, b_gate_up, w_down, b_down)
        y4 = jnp.take(y_rows, dest, axis=0).reshape(TOP_K, n, d)
        x = moe_combine(x1, tile_mod, y4, gates, mods)

    y_prompt = x[:n_p].reshape(bp, tp, d)
    y_sample = x[n_p:].reshape(bs, ts, d)
    return y_prompt, y_sample, jnp.stack(new_k, axis=1), jnp.stack(new_v, axis=1)
```

```python
import functools
import math

import numpy as np
import jax
import jax.numpy as jnp
from jax import lax
from jax.experimental import pallas as pl
from jax.experimental.pallas import tpu as pltpu

F32 = jnp.float32
BF16 = jnp.bfloat16

GRID_W = 64
HEAD_DIM = 64
V_DIM = 2 * HEAD_DIM
N_HEADS = 4
ATTN_W = N_HEADS * V_DIM
POOL_WINDOWS = (2, 4, 8, 16)
POOL_GROUP = 64
POOL_W = 256
FOUR_GROUPS = 4
FOUR_GROUP = 64
FOUR_W = 256
N_EXPERTS = 32
TOP_K = 4
SWIGLU_LIMIT = 7.0
SWIGLU_ALPHA = 1.702
ROPE_THETA = 10000.0
RMS_EPS = 1e-6

LANES = 128
TOKEN_TILE = 256
POOL_HALO = 8
KV_CHUNK = 256
ATTN_STEP_CHUNKS = 2
EXPERT_BLOCK = 256
SEG_ALIGN = 8
TILE_SLOTS = TOKEN_TILE * TOP_K + N_EXPERTS * SEG_ALIGN
DFT_ROW_TILE = 512
MOD_ROWS = 16
VMEM_LIMIT = 56 * 1024 * 1024


def _dot(a, b):
    return jnp.dot(a, b, preferred_element_type=F32)


def _split_bf16(x):
    hi = x.astype(BF16)
    lo = (x - hi.astype(F32)).astype(BF16)
    return hi, lo


def _cparams(sem):
    return pltpu.CompilerParams(dimension_semantics=sem, vmem_limit_bytes=VMEM_LIMIT)


def _mod_kernel(cond_ref, w_ref, b_ref, o_ref):
    c = cond_ref[...]
    a = c * jax.nn.sigmoid(c)
    ah, al = _split_bf16(a)
    wh, wl = _split_bf16(w_ref[0])
    o_ref[0] = _dot(ah, wh) + _dot(ah, wl) + _dot(al, wh) + b_ref[0]


def modulation_all(cond, ada_w, ada_b):
    depth, d, d6 = ada_w.shape
    tn = d6 // 4
    return pl.pallas_call(
        _mod_kernel,
        out_shape=jax.ShapeDtypeStruct((depth, MOD_ROWS, d6), F32),
        grid=(depth, d6 // tn),
        in_specs=[
            pl.BlockSpec((MOD_ROWS, d), lambda l, j: (0, 0)),
            pl.BlockSpec((1, d, tn), lambda l, j: (l, 0, j)),
            pl.BlockSpec((1, 1, tn), lambda l, j: (l, 0, j)),
        ],
        out_specs=pl.BlockSpec((1, MOD_ROWS, tn), lambda l, j: (l, 0, j)),
        compiler_params=_cparams(("parallel", "parallel")),
        name="modulation",
    )(cond, ada_w, ada_b.reshape(depth, 1, d6))


def _rms_mod(x, g, scale, shift):
    ms = jnp.mean(x * x, axis=-1, keepdims=True)
    return x * lax.rsqrt(ms + RMS_EPS) * g * (1.0 + scale) + shift


def _pre_mixer_kernel(tmod_ref, x_ref, g_ref, mod_ref, w_ref, ab_ref, proj_ref, za_ref, zb_ref):
    del tmod_ref
    h = _rms_mod(x_ref[...], g_ref[...], mod_ref[0, 1], mod_ref[0, 0])
    proj = _dot(h.astype(BF16), w_ref[...])
    nproj = proj_ref.shape[1]
    proj_ref[...] = proj[:, :nproj]
    z = _dot(proj[:, nproj:].astype(BF16), ab_ref[...])
    za_ref[...] = z[:, :FOUR_W].astype(BF16)
    zb_ref[...] = z[:, FOUR_W:].astype(BF16)


def pre_mixer(x, tile_mod, n1, mods, w_in_b, ab_b):
    n, d = x.shape
    in_w = w_in_b.shape[1]
    nproj = in_w - FOUR_W
    tm = TOKEN_TILE
    return pl.pallas_call(
        _pre_mixer_kernel,
        out_shape=(jax.ShapeDtypeStruct((n, nproj), F32),
                   jax.ShapeDtypeStruct((n, FOUR_W), BF16),
                   jax.ShapeDtypeStruct((n, FOUR_W), BF16)),
        grid_spec=pltpu.PrefetchScalarGridSpec(
            num_scalar_prefetch=1,
            grid=(n // tm,),
            in_specs=[
                pl.BlockSpec((tm, d), lambda i, tmod: (i, 0)),
                pl.BlockSpec((1, d), lambda i, tmod: (0, 0)),
                pl.BlockSpec((1, 6, 1, d), lambda i, tmod: (tmod[i], 0, 0, 0)),
                pl.BlockSpec((d, in_w), lambda i, tmod: (0, 0)),
                pl.BlockSpec((FOUR_W, 2 * FOUR_W), lambda i, tmod: (0, 0)),
            ],
            out_specs=(pl.BlockSpec((tm, nproj), lambda i, tmod: (i, 0)),
                       pl.BlockSpec((tm, FOUR_W), lambda i, tmod: (i, 0)),
                       pl.BlockSpec((tm, FOUR_W), lambda i, tmod: (i, 0))),
        ),
        compiler_params=_cparams(("parallel",)),
        name="pre_mixer",
    )(tile_mod, x, n1.reshape(1, d), mods, w_in_b, ab_b)


def _group_rms(x, gmat, g):
    sh, sl = _split_bf16(x * x)
    ss = _dot(sh, gmat) + _dot(sl, gmat)
    return x * lax.rsqrt(ss * (1.0 / HEAD_DIM) + RMS_EPS) * g


def _rope(y, cos, sin_signed):
    w = y.shape[1]
    q = HEAD_DIM // 4
    lane = lax.broadcasted_iota(jnp.int32, y.shape, 1)
    partner = jnp.where(lane % (2 * q) < q, pltpu.roll(y, w - q, axis=1), pltpu.roll(y, q, axis=1))
    return y * cos + partner * sin_signed


def _qkv_kernel(q_ref, k_ref, v_ref, gmat_ref, qg_ref, kg_ref, *rest, rope, emit_kn):
    if rope:
        cos_ref, sin_ref, *outs = rest
    else:
        outs = rest
    qt_ref, kk_ref, vt_ref = outs[:3]
    gmat = gmat_ref[...]
    qn = _group_rms(q_ref[...], gmat, qg_ref[...])
    kn = _group_rms(k_ref[...], gmat, kg_ref[...])
    if emit_kn:
        outs[3][0] = kn
    if rope:
        cos = cos_ref[...]
        sin = sin_ref[...]
        qn = _rope(qn, cos, sin)
        kn = _rope(kn, cos, sin)
    qt_ref[0] = (qn * (HEAD_DIM ** -0.5 * math.log2(math.e))).T.astype(BF16)
    kk_ref[0] = kn.astype(BF16)
    vt_ref[0, 0] = v_ref[...].T.astype(BF16)


def qkv_prep(proj, row_off, b, t, gmat, qg, kg, cos, sin, *, rope, emit_kn):
    tm = TOKEN_TILE
    nt = t // tm
    off = row_off // tm
    w = ATTN_W
    row = lambda bi, j: (off + bi * nt + j)
    in_specs = [
        pl.BlockSpec((tm, w), lambda bi, j: (row(bi, j), 0)),
        pl.BlockSpec((tm, w), lambda bi, j: (row(bi, j), 1)),
        pl.BlockSpec((tm, w), lambda bi, j: (row(bi, j), 2)),
        pl.BlockSpec((w, w), lambda bi, j: (0, 0)),
        pl.BlockSpec((1, w), lambda bi, j: (0, 0)),
        pl.BlockSpec((1, w), lambda bi, j: (0, 0)),
    ]
    args = [proj, proj, proj, gmat, qg, kg]
    if rope:
        in_specs += [pl.BlockSpec((tm, w), lambda bi, j: (j, 0)),
                     pl.BlockSpec((tm, w), lambda bi, j: (j, 0))]
        args += [cos, sin]
    out_shape = [jax.ShapeDtypeStruct((b, w, t), BF16),
                 jax.ShapeDtypeStruct((b, t, w), BF16),
                 jax.ShapeDtypeStruct((b, nt, w, tm), BF16)]
    out_specs = [pl.BlockSpec((1, w, tm), lambda bi, j: (bi, 0, j)),
                 pl.BlockSpec((1, tm, w), lambda bi, j: (bi, j, 0)),
                 pl.BlockSpec((1, 1, w, tm), lambda bi, j: (bi, j, 0, 0))]
    if emit_kn:
        out_shape.append(jax.ShapeDtypeStruct((b, t, w), F32))
        out_specs.append(pl.BlockSpec((1, tm, w), lambda bi, j: (bi, j, 0)))
    return pl.pallas_call(
        functools.partial(_qkv_kernel, rope=rope, emit_kn=emit_kn),
        out_shape=tuple(out_shape),
        grid=(b, nt),
        in_specs=in_specs,
        out_specs=tuple(out_specs),
        compiler_params=_cparams(("parallel", "parallel")),
        name="qkv_prep_rope" if rope else "qkv_prep",
    )(*args)


def _attn_kernel(lam_ref, qt_ref, k_ref, vt_ref, g_ref, o_ref, *, nk, sub, lam_init):
    q = qt_ref[0]
    row = lax.broadcasted_iota(jnp.int32, q.shape, 0)
    zero = jnp.zeros_like(q)
    qmaps = (jnp.where(row < HEAD_DIM, q, zero), jnp.where(row >= HEAD_DIM, q, zero))
    tk = sub * KV_CHUNK

    def scores(c):
        k = k_ref[0, c * tk:(c + 1) * tk, :]
        return [_dot(k, qm) for qm in qmaps]

    m_run, l_run, acc = [None, None], [None, None], [None, None]
    s_next = scores(0)
    for c in range(nk):
        s_cur = s_next
        if c + 1 < nk:
            s_next = scores(c + 1)
        for m in range(2):
            s = s_cur[m]
            m_blk = jnp.max(s, axis=0, keepdims=True)
            m_new = m_blk if c == 0 else jnp.maximum(m_run[m], m_blk)
            p = jnp.exp2(s - m_new)
            pb = p.astype(BF16)
            pv = _dot(vt_ref[0, c * sub], pb[:KV_CHUNK])
            for j in range(1, sub):
                pv = pv + _dot(vt_ref[0, c * sub + j], pb[j * KV_CHUNK:(j + 1) * KV_CHUNK])
            l_blk = jnp.sum(p, axis=0, keepdims=True)
            if c == 0:
                l_run[m], acc[m] = l_blk, pv
            else:
                alpha = jnp.exp2(m_run[m] - m_new)
                l_run[m] = alpha * l_run[m] + l_blk
                acc[m] = alpha * acc[m] + pv
            m_run[m] = m_new

    o = acc[0] / l_run[0] - lam_ref[0] * (acc[1] / l_run[1])
    ms = jnp.mean(o * o, axis=0, keepdims=True)
    o = o * lax.rsqrt(ms + RMS_EPS) * g_ref[...] * (1.0 - lam_init)
    o_ref[...] = o.T.astype(BF16)


def diff_attention(lam, qt, kk, vt, subln_g, lam_init):
    b, w, t = qt.shape
    tk = kk.shape[1]
    nk = tk // KV_CHUNK
    sub = ATTN_STEP_CHUNKS if nk % ATTN_STEP_CHUNKS == 0 else 1
    tq = TOKEN_TILE
    nq = t // tq
    hw = V_DIM
    return pl.pallas_call(
        functools.partial(_attn_kernel, nk=nk // sub, sub=sub, lam_init=lam_init),
        out_shape=jax.ShapeDtypeStruct((b * t, w), BF16),
        grid_spec=pltpu.PrefetchScalarGridSpec(
            num_scalar_prefetch=1,
            grid=(b, N_HEADS, nq),
            in_specs=[
                pl.BlockSpec((1, hw, tq), lambda bi, h, qi, lam: (bi, h, qi)),
                pl.BlockSpec((1, tk, hw), lambda bi, h, qi, lam: (bi, 0, h)),
                pl.BlockSpec((1, nk, hw, KV_CHUNK), lambda bi, h, qi, lam: (bi, 0, h, 0)),
                pl.BlockSpec((hw, 1), lambda bi, h, qi, lam: (0, 0)),
            ],
            out_specs=pl.BlockSpec((tq, hw), lambda bi, h, qi, lam: (bi * nq + qi, h)),
        ),
        compiler_params=_cparams(("parallel", "parallel", "parallel")),
        name="diff_attention",
    )(lam, qt, kk, vt, subln_g.reshape(hw, 1))


def _pool_kernel(t0_ref, tlen_ref, prev_ref, cur_ref, next_ref, pw_ref, ps_ref, o_ref):
    i = pl.program_id(0)
    t0 = t0_ref[i]
    tlen = tlen_ref[i]
    tm = cur_ref.shape[0]
    u = cur_ref[...]
    prev = jnp.where(t0 > 0, prev_ref[...], 0.0)
    nxt = jnp.where(t0 + tm < tlen, next_ref[...], 0.0)
    e = jnp.concatenate([prev, u, nxt], axis=0)
    n = e.shape[0]
    w2 = e + pltpu.roll(e, 1, axis=0)
    w4 = pltpu.roll(w2, 1, axis=0) + pltpu.roll(w2, n - 1, axis=0)
    w8 = pltpu.roll(w4, 2, axis=0) + pltpu.roll(w4, n - 2, axis=0)
    w16 = pltpu.roll(w8, 4, axis=0) + pltpu.roll(w8, n - 4, axis=0)
    sums = [w[POOL_HALO:POOL_HALO + tm] for w in (w2, w4, w8, w16)]
    lane = lax.broadcasted_iota(jnp.int32, u.shape, 1)
    grp = lane // POOL_GROUP
    tabs = t0 + lax.broadcasted_iota(jnp.int32, u.shape, 0)
    total = sums[3]
    half = jnp.full(u.shape, POOL_WINDOWS[3] // 2, jnp.int32)
    for gi in (2, 1, 0):
        total = jnp.where(grp == gi, sums[gi], total)
        half = jnp.where(grp == gi, POOL_WINDOWS[gi] // 2, half)
    cnt = jnp.minimum(tabs + half, tlen) - jnp.maximum(tabs - half, 0)
    d = total / cnt.astype(F32) - u
    o_ref[...] = (_dot(d.astype(BF16), pw_ref[...]) * ps_ref[...]).astype(BF16)


def multiscale_pool(proj, tile_t0, tile_len, pw_b, pool_scale):
    n = proj.shape[0]
    tm = TOKEN_TILE
    hb = tm // POOL_HALO
    col = (3 * ATTN_W) // POOL_W
    last = n // POOL_HALO - 1
    return pl.pallas_call(
        _pool_kernel,
        out_shape=jax.ShapeDtypeStruct((n, POOL_W), BF16),
        grid_spec=pltpu.PrefetchScalarGridSpec(
            num_scalar_prefetch=2,
            grid=(n // tm,),
            in_specs=[
                pl.BlockSpec((POOL_HALO, POOL_W), lambda i, a, b: (jnp.maximum(i * hb - 1, 0), col)),
                pl.BlockSpec((tm, POOL_W), lambda i, a, b: (i, col)),
                pl.BlockSpec((POOL_HALO, POOL_W), lambda i, a, b: (jnp.minimum((i + 1) * hb, last), col)),
                pl.BlockSpec((POOL_W, POOL_W), lambda i, a, b: (0, 0)),
                pl.BlockSpec((1, POOL_W), lambda i, a, b: (0, 0)),
            ],
            out_specs=pl.BlockSpec((tm, POOL_W), lambda i, a, b: (i, 0)),
        ),
        compiler_params=_cparams(("parallel",)),
        name="multiscale_pool",
    )(tile_t0, tile_len, proj, proj, proj, pw_b, pool_scale.reshape(1, POOL_W))


def _fourier_kernel(c_ref, s_ref, za_ref, zb_ref, o_ref):
    o_ref[0] = (_dot(c_ref[...], za_ref[0]) - _dot(s_ref[...], zb_ref[0])).astype(BF16)


def fourier_mix(cmat, smat, za, zb):
    b, t, w = za.shape
    tr = min(DFT_ROW_TILE, t)
    return pl.pallas_call(
        _fourier_kernel,
        out_shape=jax.ShapeDtypeStruct((b, t, w), BF16),
        grid=(t // tr, b),
        in_specs=[
            pl.BlockSpec((tr, t), lambda r, bi: (r, 0)),
            pl.BlockSpec((tr, t), lambda r, bi: (r, 0)),
            pl.BlockSpec((1, t, w), lambda r, bi: (bi, 0, 0)),
            pl.BlockSpec((1, t, w), lambda r, bi: (bi, 0, 0)),
        ],
        out_specs=pl.BlockSpec((1, tr, w), lambda r, bi: (bi, r, 0)),
        compiler_params=_cparams(("parallel", "parallel")),
        name="fourier_mix",
    )(cmat, smat, za, zb)


def _post_mixer_kernel(tmod_ref, x_ref, a_ref, p_ref, f_ref, wo_ref, mod_ref, g2_ref, rw_ref, rb_ref,
                       tri_ref, low_ref, x1_ref, h2_ref, pos_ref, gp_ref, cnt_ref):
    del tmod_ref
    mixed = jnp.concatenate([a_ref[...], p_ref[...], f_ref[...]], axis=1)
    x1 = x_ref[...] + mod_ref[0, 2] * _dot(mixed, wo_ref[...])
    x1_ref[...] = x1
    h2 = _rms_mod(x1, g2_ref[...], mod_ref[0, 4], mod_ref[0, 3])
    hh, hl = _split_bf16(h2)
    h2_ref[...] = hh

    rw = rw_ref[...]
    rwh, rwl = _split_bf16(rw)
    logits = _dot(hh, rwh) + _dot(hh, rwl) + _dot(hl, rwh) + rb_ref[...]
    lt = logits.T
    eidx = lax.broadcasted_iota(jnp.int32, lt.shape, 0).astype(F32)
    work = lt
    vals, idxs, hots = [], [], []
    for _ in range(TOP_K):
        mx = jnp.max(work, axis=0, keepdims=True)
        ix = jnp.min(jnp.where(work == mx, eidx, float(LANES)), axis=0, keepdims=True)
        hot = eidx == ix
        work = jnp.where(hot, -jnp.inf, work)
        vals.append(mx)
        idxs.append(ix)
        hots.append(hot)
    exps = [jnp.exp(v - vals[0]) for v in vals]
    denom = exps[0] + exps[1] + exps[2] + exps[3]
    sel = (hots[0] | hots[1] | hots[2] | hots[3]).astype(F32)
    selb = sel.astype(BF16)
    prefix = _dot(selb, tri_ref[...])
    cnt_col = jnp.sum(sel, axis=1, keepdims=True)
    seg_col = jnp.ceil(cnt_col * (1.0 / SEG_ALIGN)) * SEG_ALIGN
    seg_start = _dot(low_ref[...], jnp.broadcast_to(seg_col, (LANES, LANES)).astype(BF16))[:, 0:1]
    slot = prefix + seg_start
    tm = lt.shape[1]
    pos = [jnp.sum(jnp.where(h, slot, 0.0), axis=0, keepdims=True) for h in hots]
    pos_ref[...] = jnp.concatenate(pos + [jnp.zeros((8 - TOP_K, tm), F32)], axis=0).astype(jnp.int32)
    gp_t = jnp.concatenate([e / denom for e in exps] + pos
                           + [jnp.zeros((LANES - 2 * TOP_K, tm), F32)], axis=0)
    gp_ref[...] = gp_t.T
    cnt_ref[...] = lax.dot_general(jnp.ones((8, tm), BF16), selb, (((1,), (1,)), ((), ())),
                                   preferred_element_type=F32)


def post_mixer(x, tile_mod, attn, pool, four, wo_b, mods, n2, rw_pad, rb_pad, tri, low):
    n, d = x.shape
    tm = TOKEN_TILE
    tok = lambda i, tmod: (i, 0)
    const = lambda i, tmod: (0, 0)
    lane_tok = lambda i, tmod: (0, i)
    return pl.pallas_call(
        _post_mixer_kernel,
        out_shape=(jax.ShapeDtypeStruct((n, d), F32),
                   jax.ShapeDtypeStruct((n, d), BF16),
                   jax.ShapeDtypeStruct((8, n), jnp.int32),
                   jax.ShapeDtypeStruct((n, LANES), F32),
                   jax.ShapeDtypeStruct((n // tm * 8, LANES), F32)),
        grid_spec=pltpu.PrefetchScalarGridSpec(
            num_scalar_prefetch=1,
            grid=(n // tm,),
            in_specs=[
                pl.BlockSpec((tm, d), tok),
                pl.BlockSpec((tm, ATTN_W), tok),
                pl.BlockSpec((tm, POOL_W), tok),
                pl.BlockSpec((tm, FOUR_W), tok),
                pl.BlockSpec((d, d), const),
                pl.BlockSpec((1, 6, 1, d), lambda i, tmod: (tmod[i], 0, 0, 0)),
                pl.BlockSpec((1, d), const),
                pl.BlockSpec((d, LANES), const),
                pl.BlockSpec((1, LANES), const),
                pl.BlockSpec((tm, tm), const),
                pl.BlockSpec((LANES, LANES), const),
            ],
            out_specs=(pl.BlockSpec((tm, d), tok),
                       pl.BlockSpec((tm, d), tok),
                       pl.BlockSpec((8, tm), lane_tok),
                       pl.BlockSpec((tm, LANES), tok),
                       pl.BlockSpec((8, LANES), tok)),
        ),
        compiler_params=_cparams(("parallel",)),
        name="post_mixer_router",
    )(tile_mod, x, attn, pool, four, wo_b, mods, n2.reshape(1, d), rw_pad, rb_pad, tri, low)


def _slot_copy_loops(i, off_ref, gbase_ref, nchunk_ref, start_chunk):
    def per_expert(e, carry):
        idx = i * N_EXPERTS + e
        local0 = off_ref[idx]
        global0 = gbase_ref[idx]

        def per_chunk(j, c):
            start_chunk(pl.multiple_of(local0 + j * SEG_ALIGN, SEG_ALIGN),
                        pl.multiple_of(global0 + j * SEG_ALIGN, SEG_ALIGN))
            return c

        return lax.fori_loop(0, nchunk_ref[idx], per_chunk, carry)

    lax.fori_loop(0, N_EXPERTS, per_expert, 0)


def _dispatch_kernel(off_ref, gbase_ref, nchunk_ref, tchunks_ref, npad_ref, padbase_ref, nused_ref,
                     h2_ref, pos_ref, xrows_ref, xs, zbuf, sem):
    i = pl.program_id(0)
    last = pl.num_programs(0) - 1
    slot = i % 2
    pos = pos_ref[...]
    rows = lax.broadcasted_iota(jnp.int32, (xs.shape[1], pos.shape[1]), 0)
    hit = rows == pos[0:1]
    for k in range(1, TOP_K):
        hit = hit | (rows == pos[k:k + 1])
    xs[slot] = _dot(hit.astype(BF16), h2_ref[...])

    def chunk_copy(s, src, dst):
        return pltpu.make_async_copy(xs.at[s, pl.ds(src, SEG_ALIGN)],
                                     xrows_ref.at[pl.ds(dst, SEG_ALIGN)], sem.at[s])

    def wait_chunks(s, count):
        def wait_one(j, c):
            chunk_copy(s, 0, 0).wait()
            return c
        lax.fori_loop(0, count, wait_one, 0)

    _slot_copy_loops(i, off_ref, gbase_ref, nchunk_ref, lambda src, dst: chunk_copy(slot, src, dst).start())

    @pl.when(i > 0)
    def _():
        wait_chunks(1 - slot, tchunks_ref[jnp.maximum(i - 1, 0)])

    @pl.when(i == last)
    def _():
        wait_chunks(slot, tchunks_ref[i])
        zbuf[...] = jnp.zeros(zbuf.shape, zbuf.dtype)

        def zero_copy(dst, rows):
            return pltpu.make_async_copy(zbuf.at[pl.ds(0, rows)], xrows_ref.at[pl.ds(dst, rows)],
                                         sem.at[slot])

        def per_expert(e, total):
            def per_chunk(j, c):
                zero_copy(pl.multiple_of(padbase_ref[e] + j * SEG_ALIGN, SEG_ALIGN), SEG_ALIGN).start()
                return c
            lax.fori_loop(0, npad_ref[e], per_chunk, 0)
            return total + npad_ref[e]

        total = lax.fori_loop(0, N_EXPERTS, per_expert, 0)

        def wait_zero(j, c):
            zero_copy(0, SEG_ALIGN).wait()
            return c

        lax.fori_loop(0, total, wait_zero, 0)

        bm = zbuf.shape[0]
        n_blocks = xrows_ref.shape[0] // bm

        def zero_block(b, c):
            zero_copy(pl.multiple_of(b * bm, bm), bm).start()
            return c

        def wait_block(b, c):
            zero_copy(0, bm).wait()
            return c

        lax.fori_loop(nused_ref[0], n_blocks, zero_block, 0)
        lax.fori_loop(nused_ref[0], n_blocks, wait_block, 0)


def moe_dispatch(h2, pos, tables, n_rows):
    n, d = h2.shape
    tm = TOKEN_TILE
    return pl.pallas_call(
        _dispatch_kernel,
        out_shape=jax.ShapeDtypeStruct((n_rows, d), F32),
        grid_spec=pltpu.PrefetchScalarGridSpec(
            num_scalar_prefetch=7,
            grid=(n // tm,),
            in_specs=[
                pl.BlockSpec((tm, d), lambda i, *_: (i, 0)),
                pl.BlockSpec((8, tm), lambda i, *_: (0, i)),
            ],
            out_specs=pl.BlockSpec(memory_space=pl.ANY),
            scratch_shapes=[pltpu.VMEM((2, TILE_SLOTS, d), F32),
                            pltpu.VMEM((EXPERT_BLOCK, d), F32),
                            pltpu.SemaphoreType.DMA((2,))],
        ),
        compiler_params=_cparams(("arbitrary",)),
        name="moe_dispatch",
    )(tables["off"], tables["gbase"], tables["nchunk"], tables["tchunks"], tables["npad"],
      tables["padbase"], tables["n_used"], h2, pos)


def _expert_kernel(be_ref, nused_ref, x_ref, wgu_ref, bgu_ref, wd_ref, bd_ref, y_ref, wgu_s, wd_s):
    b = pl.program_id(0)

    @pl.when((b == 0) | (be_ref[b] != be_ref[jnp.maximum(b - 1, 0)]))
    def _():
        wgu_s[...] = wgu_ref[0, 0].astype(BF16)
        wd_s[...] = wd_ref[0, 0].astype(BF16)

    @pl.when(b < nused_ref[0])
    def _():
        gu = _dot(x_ref[...].astype(BF16), wgu_s[...]) + bgu_ref[0]
        f = gu.shape[1] // 2
        glu = jnp.minimum(gu[:, :f], SWIGLU_LIMIT)
        lin = jnp.clip(gu[:, f:], -SWIGLU_LIMIT, SWIGLU_LIMIT)
        act = glu * jax.nn.sigmoid(SWIGLU_ALPHA * glu) * (lin + 1.0)
        y_ref[...] = _dot(act.astype(BF16), wd_s[...]) + bd_ref[0]

    @pl.when(b >= nused_ref[0])
    def _():
        y_ref[...] = jnp.zeros(y_ref.shape, y_ref.dtype)


def expert_ffn(block_expert, n_used, x_rows, layer, wgu, bgu, wd, bd):
    n_rows, d = x_rows.shape
    depth, e, _, f2 = wgu.shape
    bm = EXPERT_BLOCK
    return pl.pallas_call(
        _expert_kernel,
        out_shape=jax.ShapeDtypeStruct((n_rows, d), F32),
        grid_spec=pltpu.PrefetchScalarGridSpec(
            num_scalar_prefetch=2,
            grid=(n_rows // bm,),
            in_specs=[
                pl.BlockSpec((bm, d), lambda b, be, nu: (jnp.minimum(b, nu[0] - 1), 0)),
                pl.BlockSpec((1, 1, d, f2), lambda b, be, nu: (layer, be[b], 0, 0)),
                pl.BlockSpec((1, 1, f2), lambda b, be, nu: (layer * e + be[b], 0, 0)),
                pl.BlockSpec((1, 1, f2 // 2, d), lambda b, be, nu: (layer, be[b], 0, 0)),
                pl.BlockSpec((1, 1, d), lambda b, be, nu: (layer * e + be[b], 0, 0)),
            ],
            out_specs=pl.BlockSpec((bm, d), lambda b, be, nu: (b, 0)),
            scratch_shapes=[pltpu.VMEM((d, f2), BF16), pltpu.VMEM((f2 // 2, d), BF16)],
        ),
        compiler_params=_cparams(("arbitrary",)),
        name="expert_ffn",
    )(block_expert, n_used, x_rows, wgu, bgu.reshape(depth * e, 1, f2), wd, bd.reshape(depth * e, 1, d))


def _combine_kernel(tmod_ref, off_ref, gbase_ref, nchunk_ref, tchunks_ref,
                    x_ref, gp_ref, mod_ref, yrows_ref, o_ref, ys, sem):
    del tmod_ref
    i = pl.program_id(0)
    last = pl.num_programs(0) - 1
    slot = i % 2

    def chunk_copy(s, src, dst):
        return pltpu.make_async_copy(yrows_ref.at[pl.ds(src, SEG_ALIGN)],
                                     ys.at[s, pl.ds(dst, SEG_ALIGN)], sem.at[s])

    def fetch(tile, s):
        _slot_copy_loops(tile, off_ref, gbase_ref, nchunk_ref,
                         lambda local, glob: chunk_copy(s, glob, local).start())

    @pl.when(i == 0)
    def _():
        ys[...] = jnp.zeros(ys.shape, ys.dtype)
        fetch(0, 0)

    @pl.when(i < last)
    def _():
        fetch(i + 1, 1 - slot)

    def wait_one(j, c):
        chunk_copy(slot, 0, 0).wait()
        return c

    lax.fori_loop(0, tchunks_ref[i], wait_one, 0)

    gp = gp_ref[...]
    lanes = lax.broadcasted_iota(jnp.int32, (gp.shape[0], ys.shape[1]), 1).astype(F32)
    wmat = jnp.where(lanes == gp[:, TOP_K:TOP_K + 1], gp[:, 0:1], 0.0)
    for k in range(1, TOP_K):
        wmat = wmat + jnp.where(lanes == gp[:, TOP_K + k:TOP_K + k + 1], gp[:, k:k + 1], 0.0)
    wh, wl = _split_bf16(wmat)
    yb = ys[slot].astype(BF16)
    o_ref[...] = x_ref[...] + mod_ref[0, 5] * (_dot(wh, yb) + _dot(wl, yb))


def moe_combine(x1, tile_mod, y_rows, gp, mods, tables):
    n, d = x1.shape
    tm = TOKEN_TILE
    return pl.pallas_call(
        _combine_kernel,
        out_shape=jax.ShapeDtypeStruct((n, d), F32),
        grid_spec=pltpu.PrefetchScalarGridSpec(
            num_scalar_prefetch=5,
            grid=(n // tm,),
            in_specs=[
                pl.BlockSpec((tm, d), lambda i, *_: (i, 0)),
                pl.BlockSpec((tm, LANES), lambda i, *_: (i, 0)),
                pl.BlockSpec((1, 6, 1, d), lambda i, tmod, *_: (tmod[i], 0, 0, 0)),
                pl.BlockSpec(memory_space=pl.ANY),
            ],
            out_specs=pl.BlockSpec((tm, d), lambda i, *_: (i, 0)),
            scratch_shapes=[pltpu.VMEM((2, TILE_SLOTS, d), F32),
                            pltpu.SemaphoreType.DMA((2,))],
        ),
        compiler_params=_cparams(("arbitrary",)),
        name="moe_combine",
    )(tile_mod, tables["off"], tables["gbase"], tables["nchunk"], tables["tchunks"],
      x1, gp, mods, y_rows)


def _rope_tables(t):
    quarter = HEAD_DIM // 4
    lane = jnp.arange(ATTN_W)
    inv_freq = ROPE_THETA ** (-(lane % quarter).astype(F32) / quarter)
    pos_t = jnp.arange(t)
    pos = jnp.where((lane % HEAD_DIM)[None, :] < HEAD_DIM // 2,
                    (pos_t // GRID_W)[:, None], (pos_t % GRID_W)[:, None]).astype(F32)
    ang = pos * inv_freq[None, :]
    sign = jnp.where(lane % (2 * quarter) < quarter, -1.0, 1.0).astype(F32)
    return jnp.cos(ang), jnp.sin(ang) * sign[None, :]


def _dft_tables(t):
    k = jnp.arange(t, dtype=jnp.int32)
    r = (k[:, None] * k[None, :]) % t
    ang = r.astype(F32) * (2.0 * math.pi / t)
    scale = 1.0 / math.sqrt(t)
    return (jnp.cos(ang) * scale).astype(BF16), (jnp.sin(ang) * scale).astype(BF16)


def _block_diag(blocks):
    g, c, d = blocks.shape
    eye = jnp.eye(g, dtype=blocks.dtype)
    return (eye[:, None, :, None] * blocks[:, :, None, :]).reshape(g * c, g * d)


def _fourier_weights(fourier_w_l):
    c = jnp.arange(FOUR_GROUP, dtype=jnp.int32)
    ang = ((c[:, None] * c[None, :]) % FOUR_GROUP).astype(F32) * (2.0 * math.pi / FOUR_GROUP)
    scale = 1.0 / math.sqrt(FOUR_GROUP)
    hp = lax.Precision.HIGHEST
    a = jnp.einsum("ck,gkd->gcd", jnp.cos(ang) * scale, fourier_w_l, precision=hp)
    b = jnp.einsum("ck,gkd->gcd", jnp.sin(ang) * scale, fourier_w_l, precision=hp)
    return jnp.concatenate([_block_diag(a), _block_diag(b)], axis=1)


def kernel(x_prompt, x_sample, c, c_ctx, cache_k, cache_v, ada_w, ada_b, norm1_g, norm2_g, w_in,
           q_norm_g, k_norm_g, lambda_qk, subln_g, pool_w, pool_scale, fourier_w, w_out,
           router_w, router_b, w_gate_up, b_gate_up, w_down, b_down):
    bp, tp, d = x_prompt.shape
    bs, ts, _ = x_sample.shape
    depth = ada_w.shape[0]
    past = cache_k.shape[2]
    tm = TOKEN_TILE
    n_p, n_s = bp * tp, bs * ts
    n = n_p + n_s
    assert tp % tm == 0 and ts % tm == 0 and past % KV_CHUNK == 0 and bs + 1 <= MOD_ROWS

    tile_mod = np.concatenate([np.zeros(n_p // tm, np.int32),
                               1 + np.repeat(np.arange(bs, dtype=np.int32), ts // tm)])
    tile_t0 = np.concatenate([np.tile(np.arange(0, tp, tm, dtype=np.int32), bp),
                              np.tile(np.arange(0, ts, tm, dtype=np.int32), bs)])
    tile_len = np.concatenate([np.full(n_p // tm, tp, np.int32), np.full(n_s // tm, ts, np.int32)])
    tile_mod, tile_t0, tile_len = map(jnp.asarray, (tile_mod, tile_t0, tile_len))

    cond = jnp.zeros((MOD_ROWS, d), F32).at[0].set(c_ctx).at[1:1 + bs].set(c)
    mods_all = modulation_all(cond, ada_w, ada_b).reshape(depth, MOD_ROWS, 6, 1, d)

    cos_t, sin_t = _rope_tables(ts)
    dft_p = _dft_tables(tp)
    dft_s = _dft_tables(ts)
    gmat = _block_diag(jnp.ones((ATTN_W // HEAD_DIM, HEAD_DIM, HEAD_DIM), BF16))
    tri = (jnp.arange(tm)[:, None] < jnp.arange(tm)[None, :]).astype(BF16)
    tile_w = ATTN_W // HEAD_DIM

    low = (jnp.arange(LANES)[:, None] > jnp.arange(LANES)[None, :]).astype(BF16)
    n_tiles = n // tm
    bm = EXPERT_BLOCK
    max_rows = n * TOP_K + n_tiles * N_EXPERTS * (SEG_ALIGN - 1)
    n_blocks = -(-max_rows // bm) + N_EXPERTS
    n_rows = n_blocks * bm

    x = jnp.concatenate([x_prompt.reshape(n_p, d), x_sample.reshape(n_s, d)], axis=0)
    new_k, new_v = [], []
    for l in range(depth):
        lam_init = 0.8 - 0.6 * math.exp(-0.3 * l)
        mods = mods_all[l]
        lf = lambda_qk[l]
        lam = (jnp.exp(jnp.sum(lf[0] * lf[1])) - jnp.exp(jnp.sum(lf[2] * lf[3])) + lam_init).reshape(1)

        ab = _fourier_weights(fourier_w[l]).astype(BF16)
        proj, za, zb = pre_mixer(x, tile_mod, norm1_g[l], mods, w_in[l].astype(BF16), ab)

        qg = jnp.tile(q_norm_g[l], tile_w).reshape(1, ATTN_W)
        kg = jnp.tile(k_norm_g[l], tile_w).reshape(1, ATTN_W)

        qt_p, kk_p, vt_p, kn_p = qkv_prep(proj, 0, bp, tp, gmat, qg, kg, None, None,
                                          rope=False, emit_kn=True)
        attn_p = diff_attention(lam, qt_p, kk_p, vt_p, subln_g[l], lam_init)
        new_k.append(kn_p.reshape(bp, tp, N_HEADS, 2, HEAD_DIM))
        new_v.append(proj[:n_p, 2 * ATTN_W:3 * ATTN_W].reshape(bp, tp, N_HEADS, V_DIM))

        qt_s, kk_s, vt_s = qkv_prep(proj, n_p, bs, ts, gmat, qg, kg, cos_t, sin_t,
                                    rope=True, emit_kn=False)
        ck = cache_k[:, l].reshape(bs, past, ATTN_W).astype(BF16)
        cv = cache_v[:, l].reshape(bs, past // KV_CHUNK, KV_CHUNK, ATTN_W).astype(BF16)
        kk_all = jnp.concatenate([kk_s, ck], axis=1)
        vt_all = jnp.concatenate([vt_s, cv.transpose(0, 1, 3, 2)], axis=1)
        attn_s = diff_attention(lam, qt_s, kk_all, vt_all, subln_g[l], lam_init)
        attn = jnp.concatenate([attn_p, attn_s], axis=0)

        pool = multiscale_pool(proj, tile_t0, tile_len, _block_diag(pool_w[l]).astype(BF16),
                               pool_scale[l])

        four_p = fourier_mix(dft_p[0], dft_p[1],
                             za[:n_p].reshape(bp, tp, FOUR_W), zb[:n_p].reshape(bp, tp, FOUR_W))
        four_s = fourier_mix(dft_s[0], dft_s[1],
                             za[n_p:].reshape(bs, ts, FOUR_W), zb[n_p:].reshape(bs, ts, FOUR_W))
        four = jnp.concatenate([four_p.reshape(n_p, FOUR_W), four_s.reshape(n_s, FOUR_W)], axis=0)

        rw_pad = jnp.zeros((d, LANES), F32).at[:, :N_EXPERTS].set(router_w[l])
        rb_pad = jnp.full((1, LANES), -1e30, F32).at[0, :N_EXPERTS].set(router_b[l])
        x1, h2, pos, gp, counts = post_mixer(
            x, tile_mod, attn, pool, four, w_out[l].astype(BF16), mods, norm2_g[l], rw_pad, rb_pad,
            tri, low)

        cnt = counts.reshape(n_tiles, 8, LANES)[:, 0, :N_EXPERTS].astype(jnp.int32)
        seg = (cnt + SEG_ALIGN - 1) // SEG_ALIGN * SEG_ALIGN
        off = jnp.cumsum(seg, axis=1) - seg
        rows_e = jnp.sum(seg, axis=0)
        padded = (rows_e + bm - 1) // bm * bm
        pad_end = jnp.cumsum(padded)
        pad_start = pad_end - padded
        gbase = pad_start[None, :] + jnp.cumsum(seg, axis=0) - seg
        nchunk = seg // SEG_ALIGN
        n_used = (pad_end[-1] // bm).astype(jnp.int32).reshape(1)
        tables = {
            "off": off.reshape(-1), "gbase": gbase.reshape(-1), "nchunk": nchunk.reshape(-1),
            "tchunks": jnp.sum(nchunk, axis=1), "npad": (padded - rows_e) // SEG_ALIGN,
            "padbase": pad_start + rows_e, "n_used": n_used,
        }
        block_row0 = jnp.arange(n_blocks, dtype=jnp.int32) * bm
        block_expert = jnp.minimum(
            jnp.sum((pad_end[None, :] <= block_row0[:, None]).astype(jnp.int32), axis=1), N_EXPERTS - 1)

        x_rows = moe_dispatch(h2, pos, tables, n_rows)
        y_rows = expert_ffn(block_expert, n_used, x_rows, l, w_gate_up, b_gate_up, w_down, b_down)
        x = moe_combine(x1, tile_mod, y_rows, gp, mods, tables)

    y_prompt = x[:n_p].reshape(bp, tp, d)
    y_sample = x[n_p:].reshape(bs, ts, d)
    return y_prompt, y_sample, jnp.stack(new_k, axis=1), jnp.stack(new_v, axis=1)
```

```python
import functools
import math

import numpy as np
import jax
import jax.numpy as jnp
from jax import lax
from jax.experimental import pallas as pl
from jax.experimental.pallas import tpu as pltpu

F32 = jnp.float32
BF16 = jnp.bfloat16

GRID_W = 64
HEAD_DIM = 64
V_DIM = 2 * HEAD_DIM
N_HEADS = 4
ATTN_W = N_HEADS * V_DIM
POOL_WINDOWS = (2, 4, 8, 16)
POOL_GROUP = 64
POOL_W = 256
FOUR_GROUPS = 4
FOUR_GROUP = 64
FOUR_W = 256
N_EXPERTS = 32
TOP_K = 4
SWIGLU_LIMIT = 7.0
SWIGLU_ALPHA = 1.702
ROPE_THETA = 10000.0
RMS_EPS = 1e-6

LANES = 128
TOKEN_TILE = 256
POOL_HALO = 8
KV_CHUNK = 256
ATTN_STEP_CHUNKS = 2
V_ONES_ROWS = 16
V_AUG = V_DIM + V_ONES_ROWS
EXPERT_BLOCK = 256
SEG_ALIGN = 8
TILE_SLOTS = TOKEN_TILE * TOP_K + N_EXPERTS * SEG_ALIGN
COPY_ROWS = (4 * SEG_ALIGN, SEG_ALIGN)
DFT_ROW_TILE = 512
MOD_ROWS = 16
VMEM_LIMIT = 56 * 1024 * 1024


def _dot(a, b):
    return jnp.dot(a, b, preferred_element_type=F32)


def _split_bf16(x):
    hi = x.astype(BF16)
    lo = (x - hi.astype(F32)).astype(BF16)
    return hi, lo


def _cparams(sem):
    return pltpu.CompilerParams(dimension_semantics=sem, vmem_limit_bytes=VMEM_LIMIT)


def _mod_kernel(cond_ref, w_ref, b_ref, o_ref):
    c = cond_ref[...]
    a = c * jax.nn.sigmoid(c)
    ah, al = _split_bf16(a)
    wh, wl = _split_bf16(w_ref[0])
    o_ref[0] = _dot(ah, wh) + _dot(ah, wl) + _dot(al, wh) + b_ref[0]


def modulation_all(cond, ada_w, ada_b):
    depth, d, d6 = ada_w.shape
    tn = d6 // 4
    return pl.pallas_call(
        _mod_kernel,
        out_shape=jax.ShapeDtypeStruct((depth, MOD_ROWS, d6), F32),
        grid=(depth, d6 // tn),
        in_specs=[
            pl.BlockSpec((MOD_ROWS, d), lambda l, j: (0, 0)),
            pl.BlockSpec((1, d, tn), lambda l, j: (l, 0, j)),
            pl.BlockSpec((1, 1, tn), lambda l, j: (l, 0, j)),
        ],
        out_specs=pl.BlockSpec((1, MOD_ROWS, tn), lambda l, j: (l, 0, j)),
        compiler_params=_cparams(("parallel", "parallel")),
        name="modulation",
    )(cond, ada_w, ada_b.reshape(depth, 1, d6))


def _rms_mod(x, g, scale, shift):
    ms = jnp.mean(x * x, axis=-1, keepdims=True)
    return x * lax.rsqrt(ms + RMS_EPS) * g * (1.0 + scale) + shift


def _pre_mixer_kernel(tmod_ref, x_ref, g_ref, mod_ref, w_ref, ab_ref, proj_ref, za_ref, zb_ref):
    del tmod_ref
    h = _rms_mod(x_ref[...], g_ref[...], mod_ref[0, 1], mod_ref[0, 0])
    proj = _dot(h.astype(BF16), w_ref[...])
    nproj = proj_ref.shape[1]
    proj_ref[...] = proj[:, :nproj]
    z = _dot(proj[:, nproj:].astype(BF16), ab_ref[...])
    za_ref[...] = z[:, :FOUR_W].astype(BF16)
    zb_ref[...] = z[:, FOUR_W:].astype(BF16)


def pre_mixer(x, tile_mod, n1, mods, w_in_b, ab_b):
    n, d = x.shape
    in_w = w_in_b.shape[1]
    nproj = in_w - FOUR_W
    tm = TOKEN_TILE
    return pl.pallas_call(
        _pre_mixer_kernel,
        out_shape=(jax.ShapeDtypeStruct((n, nproj), F32),
                   jax.ShapeDtypeStruct((n, FOUR_W), BF16),
                   jax.ShapeDtypeStruct((n, FOUR_W), BF16)),
        grid_spec=pltpu.PrefetchScalarGridSpec(
            num_scalar_prefetch=1,
            grid=(n // tm,),
            in_specs=[
                pl.BlockSpec((tm, d), lambda i, tmod: (i, 0)),
                pl.BlockSpec((1, d), lambda i, tmod: (0, 0)),
                pl.BlockSpec((1, 6, 1, d), lambda i, tmod: (tmod[i], 0, 0, 0)),
                pl.BlockSpec((d, in_w), lambda i, tmod: (0, 0)),
                pl.BlockSpec((FOUR_W, 2 * FOUR_W), lambda i, tmod: (0, 0)),
            ],
            out_specs=(pl.BlockSpec((tm, nproj), lambda i, tmod: (i, 0)),
                       pl.BlockSpec((tm, FOUR_W), lambda i, tmod: (i, 0)),
                       pl.BlockSpec((tm, FOUR_W), lambda i, tmod: (i, 0))),
        ),
        compiler_params=_cparams(("parallel",)),
        name="pre_mixer",
    )(tile_mod, x, n1.reshape(1, d), mods, w_in_b, ab_b)


def _group_rms(x, gmat, g):
    ss = _dot((x * x).astype(BF16), gmat)
    return x * lax.rsqrt(ss * (1.0 / HEAD_DIM) + RMS_EPS) * g


def _rope(y, cos, sin_signed):
    w = y.shape[1]
    q = HEAD_DIM // 4
    lane = lax.broadcasted_iota(jnp.int32, y.shape, 1)
    partner = jnp.where(lane % (2 * q) < q, pltpu.roll(y, w - q, axis=1), pltpu.roll(y, q, axis=1))
    return y * cos + partner * sin_signed


def _qkv_kernel(q_ref, k_ref, v_ref, gmat_ref, qg_ref, kg_ref, *rest, rope, emit_kn):
    if rope:
        cos_ref, sin_ref, *outs = rest
    else:
        outs = rest
    qt_ref, kk_ref, vt_ref = outs[:3]
    gmat = gmat_ref[...]
    qn = _group_rms(q_ref[...], gmat, qg_ref[...])
    kn = _group_rms(k_ref[...], gmat, kg_ref[...])
    if emit_kn:
        outs[3][0] = kn
    if rope:
        cos = cos_ref[...]
        sin = sin_ref[...]
        qn = _rope(qn, cos, sin)
        kn = _rope(kn, cos, sin)
    qt_ref[0] = (qn * (HEAD_DIM ** -0.5 * math.log2(math.e))).T.astype(BF16)
    kk_ref[0] = kn.astype(BF16)
    vt = v_ref[...].T.astype(BF16)
    ones = jnp.ones((V_ONES_ROWS, vt.shape[1]), BF16)
    pieces = []
    for h in range(N_HEADS):
        pieces += [vt[h * V_DIM:(h + 1) * V_DIM], ones]
    vt_ref[0, 0] = jnp.concatenate(pieces, axis=0)


def qkv_prep(proj, row_off, b, t, gmat, qg, kg, cos, sin, *, rope, emit_kn):
    tm = TOKEN_TILE
    nt = t // tm
    off = row_off // tm
    w = ATTN_W
    row = lambda bi, j: (off + bi * nt + j)
    in_specs = [
        pl.BlockSpec((tm, w), lambda bi, j: (row(bi, j), 0)),
        pl.BlockSpec((tm, w), lambda bi, j: (row(bi, j), 1)),
        pl.BlockSpec((tm, w), lambda bi, j: (row(bi, j), 2)),
        pl.BlockSpec((w, w), lambda bi, j: (0, 0)),
        pl.BlockSpec((1, w), lambda bi, j: (0, 0)),
        pl.BlockSpec((1, w), lambda bi, j: (0, 0)),
    ]
    args = [proj, proj, proj, gmat, qg, kg]
    if rope:
        in_specs += [pl.BlockSpec((tm, w), lambda bi, j: (j, 0)),
                     pl.BlockSpec((tm, w), lambda bi, j: (j, 0))]
        args += [cos, sin]
    out_shape = [jax.ShapeDtypeStruct((b, w, t), BF16),
                 jax.ShapeDtypeStruct((b, t, w), BF16),
                 jax.ShapeDtypeStruct((b, nt, N_HEADS * V_AUG, tm), BF16)]
    out_specs = [pl.BlockSpec((1, w, tm), lambda bi, j: (bi, 0, j)),
                 pl.BlockSpec((1, tm, w), lambda bi, j: (bi, j, 0)),
                 pl.BlockSpec((1, 1, N_HEADS * V_AUG, tm), lambda bi, j: (bi, j, 0, 0))]
    if emit_kn:
        out_shape.append(jax.ShapeDtypeStruct((b, t, w), F32))
        out_specs.append(pl.BlockSpec((1, tm, w), lambda bi, j: (bi, j, 0)))
    return pl.pallas_call(
        functools.partial(_qkv_kernel, rope=rope, emit_kn=emit_kn),
        out_shape=tuple(out_shape),
        grid=(b, nt),
        in_specs=in_specs,
        out_specs=tuple(out_specs),
        compiler_params=_cparams(("parallel", "parallel")),
        name="qkv_prep_rope" if rope else "qkv_prep",
    )(*args)


def _attn_kernel(lam_ref, qt_ref, k_ref, vt_ref, *rest, n_own, n_ctx, sub, lam_init):
    if n_ctx:
        kc_ref, vc_ref, g_ref, o_ref = rest
    else:
        g_ref, o_ref = rest
    q = qt_ref[0]
    row = lax.broadcasted_iota(jnp.int32, q.shape, 0)
    zero = jnp.zeros_like(q)
    qmaps = (jnp.where(row < HEAD_DIM, q, zero), jnp.where(row >= HEAD_DIM, q, zero))
    tk = sub * KV_CHUNK
    steps = [(k_ref, vt_ref, c) for c in range(n_own)] + [(kc_ref, vc_ref, c) for c in range(n_ctx)]
    nk = len(steps)

    def scores(step):
        kr, _, c = steps[step]
        k = kr[0, c * tk:(c + 1) * tk, :]
        return [_dot(k, qm) for qm in qmaps]

    m_run, acc = [None, None], [None, None]
    s_next = scores(0)
    for c in range(nk):
        s_cur = s_next
        if c + 1 < nk:
            s_next = scores(c + 1)
        _, vr, cv = steps[c]
        for m in range(2):
            s = s_cur[m]
            m_blk = jnp.max(s, axis=0, keepdims=True)
            m_new = m_blk if c == 0 else jnp.maximum(m_run[m], m_blk)
            pb = jnp.exp2((s - m_new).astype(BF16))
            pv = _dot(vr[0, cv * sub], pb[:KV_CHUNK])
            for j in range(1, sub):
                pv = pv + _dot(vr[0, cv * sub + j], pb[j * KV_CHUNK:(j + 1) * KV_CHUNK])
            if c == 0:
                acc[m] = pv
            else:
                acc[m] = jnp.exp2(m_run[m] - m_new) * acc[m] + pv
            m_run[m] = m_new

    outs = [a[:V_DIM] / a[V_DIM:V_DIM + 1] for a in acc]
    o = outs[0] - lam_ref[0] * outs[1]
    ms = jnp.mean(o * o, axis=0, keepdims=True)
    o = o * lax.rsqrt(ms + RMS_EPS) * g_ref[...] * (1.0 - lam_init)
    o_ref[...] = o.T.astype(BF16)


def diff_attention(lam, qt, kk, vt, subln_g, lam_init, ctx=None):
    b, w, t = qt.shape
    tq = TOKEN_TILE
    nq = t // tq
    hw = V_DIM
    kv = [(kk, vt)] + ([ctx] if ctx is not None else [])
    chunks = [k.shape[1] // KV_CHUNK for k, _ in kv]
    sub = ATTN_STEP_CHUNKS if all(c % ATTN_STEP_CHUNKS == 0 for c in chunks) else 1
    in_specs = [pl.BlockSpec((1, hw, tq), lambda bi, h, qi, lam: (bi, h, qi))]
    args = [qt]
    for (k, v), nc in zip(kv, chunks):
        in_specs += [pl.BlockSpec((1, k.shape[1], hw), lambda bi, h, qi, lam: (bi, 0, h)),
                     pl.BlockSpec((1, nc, V_AUG, KV_CHUNK), lambda bi, h, qi, lam: (bi, 0, h, 0))]
        args += [k, v]
    in_specs.append(pl.BlockSpec((hw, 1), lambda bi, h, qi, lam: (0, 0)))
    return pl.pallas_call(
        functools.partial(_attn_kernel, n_own=chunks[0] // sub,
                          n_ctx=chunks[1] // sub if ctx is not None else 0, sub=sub, lam_init=lam_init),
        out_shape=jax.ShapeDtypeStruct((b * t, w), BF16),
        grid_spec=pltpu.PrefetchScalarGridSpec(
            num_scalar_prefetch=1,
            grid=(b, N_HEADS, nq),
            in_specs=in_specs,
            out_specs=pl.BlockSpec((tq, hw), lambda bi, h, qi, lam: (bi * nq + qi, h)),
        ),
        compiler_params=_cparams(("parallel", "parallel", "parallel")),
        name="diff_attention",
    )(lam, *args, subln_g.reshape(hw, 1))


def _pool_kernel(t0_ref, tlen_ref, prev_ref, cur_ref, next_ref, pw_ref, ps_ref, o_ref):
    i = pl.program_id(0)
    t0 = t0_ref[i]
    tlen = tlen_ref[i]
    tm = cur_ref.shape[0]
    u = cur_ref[...]
    prev = jnp.where(t0 > 0, prev_ref[...], 0.0)
    nxt = jnp.where(t0 + tm < tlen, next_ref[...], 0.0)
    e = jnp.concatenate([prev, u, nxt], axis=0)
    n = e.shape[0]
    w2 = e + pltpu.roll(e, 1, axis=0)
    w4 = pltpu.roll(w2, 1, axis=0) + pltpu.roll(w2, n - 1, axis=0)
    w8 = pltpu.roll(w4, 2, axis=0) + pltpu.roll(w4, n - 2, axis=0)
    w16 = pltpu.roll(w8, 4, axis=0) + pltpu.roll(w8, n - 4, axis=0)
    sums = [w[POOL_HALO:POOL_HALO + tm] for w in (w2, w4, w8, w16)]
    lane = lax.broadcasted_iota(jnp.int32, u.shape, 1)
    grp = lane // POOL_GROUP
    tabs = t0 + lax.broadcasted_iota(jnp.int32, u.shape, 0)
    total = sums[3]
    half = jnp.full(u.shape, POOL_WINDOWS[3] // 2, jnp.int32)
    for gi in (2, 1, 0):
        total = jnp.where(grp == gi, sums[gi], total)
        half = jnp.where(grp == gi, POOL_WINDOWS[gi] // 2, half)
    cnt = jnp.minimum(tabs + half, tlen) - jnp.maximum(tabs - half, 0)
    d = total / cnt.astype(F32) - u
    o_ref[...] = (_dot(d.astype(BF16), pw_ref[...]) * ps_ref[...]).astype(BF16)


def multiscale_pool(proj, tile_t0, tile_len, pw_b, pool_scale):
    n = proj.shape[0]
    tm = TOKEN_TILE
    hb = tm // POOL_HALO
    col = (3 * ATTN_W) // POOL_W
    last = n // POOL_HALO - 1
    return pl.pallas_call(
        _pool_kernel,
        out_shape=jax.ShapeDtypeStruct((n, POOL_W), BF16),
        grid_spec=pltpu.PrefetchScalarGridSpec(
            num_scalar_prefetch=2,
            grid=(n // tm,),
            in_specs=[
                pl.BlockSpec((POOL_HALO, POOL_W), lambda i, a, b: (jnp.maximum(i * hb - 1, 0), col)),
                pl.BlockSpec((tm, POOL_W), lambda i, a, b: (i, col)),
                pl.BlockSpec((POOL_HALO, POOL_W), lambda i, a, b: (jnp.minimum((i + 1) * hb, last), col)),
                pl.BlockSpec((POOL_W, POOL_W), lambda i, a, b: (0, 0)),
                pl.BlockSpec((1, POOL_W), lambda i, a, b: (0, 0)),
            ],
            out_specs=pl.BlockSpec((tm, POOL_W), lambda i, a, b: (i, 0)),
        ),
        compiler_params=_cparams(("parallel",)),
        name="multiscale_pool",
    )(tile_t0, tile_len, proj, proj, proj, pw_b, pool_scale.reshape(1, POOL_W))


def _fourier_kernel(c_ref, s_ref, za_ref, zb_ref, o_ref):
    o_ref[0] = (_dot(c_ref[...], za_ref[0]) - _dot(s_ref[...], zb_ref[0])).astype(BF16)


def fourier_mix(cmat, smat, za, zb):
    b, t, w = za.shape
    tr = min(DFT_ROW_TILE, t)
    return pl.pallas_call(
        _fourier_kernel,
        out_shape=jax.ShapeDtypeStruct((b, t, w), BF16),
        grid=(t // tr, b),
        in_specs=[
            pl.BlockSpec((tr, t), lambda r, bi: (r, 0)),
            pl.BlockSpec((tr, t), lambda r, bi: (r, 0)),
            pl.BlockSpec((1, t, w), lambda r, bi: (bi, 0, 0)),
            pl.BlockSpec((1, t, w), lambda r, bi: (bi, 0, 0)),
        ],
        out_specs=pl.BlockSpec((1, tr, w), lambda r, bi: (bi, r, 0)),
        compiler_params=_cparams(("parallel", "parallel")),
        name="fourier_mix",
    )(cmat, smat, za, zb)


def _post_mixer_kernel(tmod_ref, x_ref, ap_ref, as_ref, p_ref, fp_ref, fs_ref, wo_ref, mod_ref, g2_ref,
                       rw_ref, rb_ref, tri_ref, low_ref, x1_ref, h2_ref, pos_ref, gp_ref, cnt_ref,
                       *, first_tiles):
    del tmod_ref
    in_first = pl.program_id(0) < first_tiles
    attn = jnp.where(in_first, ap_ref[...], as_ref[...])
    four = jnp.where(in_first, fp_ref[...], fs_ref[...])
    mixed = jnp.concatenate([attn, p_ref[...], four], axis=1)
    x1 = x_ref[...] + mod_ref[0, 2] * _dot(mixed, wo_ref[...])
    x1_ref[...] = x1
    h2 = _rms_mod(x1, g2_ref[...], mod_ref[0, 4], mod_ref[0, 3])
    hh, hl = _split_bf16(h2)
    h2_ref[...] = hh

    rw = rw_ref[...]
    rwh, rwl = _split_bf16(rw)
    logits = _dot(hh, rwh) + _dot(hh, rwl) + _dot(hl, rwh) + rb_ref[...]
    lt = logits.T
    eidx = lax.broadcasted_iota(jnp.int32, lt.shape, 0).astype(F32)
    work = lt
    vals, idxs, hots = [], [], []
    for _ in range(TOP_K):
        mx = jnp.max(work, axis=0, keepdims=True)
        ix = jnp.min(jnp.where(work == mx, eidx, float(LANES)), axis=0, keepdims=True)
        hot = eidx == ix
        work = jnp.where(hot, -jnp.inf, work)
        vals.append(mx)
        idxs.append(ix)
        hots.append(hot)
    exps = [jnp.exp(v - vals[0]) for v in vals]
    denom = exps[0] + exps[1] + exps[2] + exps[3]
    sel = (hots[0] | hots[1] | hots[2] | hots[3]).astype(F32)
    selb = sel.astype(BF16)
    prefix = _dot(selb, tri_ref[...])
    cnt_col = jnp.sum(sel, axis=1, keepdims=True)
    seg_col = jnp.ceil(cnt_col * (1.0 / SEG_ALIGN)) * SEG_ALIGN
    seg_start = _dot(low_ref[...], jnp.broadcast_to(seg_col, (LANES, LANES)).astype(BF16))[:, 0:1]
    slot = prefix + seg_start
    tm = lt.shape[1]
    pos = [jnp.sum(jnp.where(h, slot, 0.0), axis=0, keepdims=True) for h in hots]
    pos_ref[...] = jnp.concatenate(pos + [jnp.zeros((8 - TOP_K, tm), F32)], axis=0).astype(jnp.int32)
    gp_t = jnp.concatenate([e / denom for e in exps] + pos
                           + [jnp.zeros((LANES - 2 * TOP_K, tm), F32)], axis=0)
    gp_ref[...] = gp_t.T
    cnt_ref[...] = lax.dot_general(jnp.ones((8, tm), BF16), selb, (((1,), (1,)), ((), ())),
                                   preferred_element_type=F32)


def post_mixer(x, tile_mod, attn, pool, four, wo_b, mods, n2, rw_pad, rb_pad, tri, low):
    n, d = x.shape
    tm = TOKEN_TILE
    first_tiles = attn[0].shape[0] // tm
    tok = lambda i, tmod: (i, 0)
    tok_first = lambda i, tmod: (jnp.minimum(i, first_tiles - 1), 0)
    tok_second = lambda i, tmod: (jnp.maximum(i - first_tiles, 0), 0)
    const = lambda i, tmod: (0, 0)
    lane_tok = lambda i, tmod: (0, i)
    return pl.pallas_call(
        functools.partial(_post_mixer_kernel, first_tiles=first_tiles),
        out_shape=(jax.ShapeDtypeStruct((n, d), F32),
                   jax.ShapeDtypeStruct((n, d), BF16),
                   jax.ShapeDtypeStruct((8, n), jnp.int32),
                   jax.ShapeDtypeStruct((n, LANES), F32),
                   jax.ShapeDtypeStruct((n // tm * 8, LANES), F32)),
        grid_spec=pltpu.PrefetchScalarGridSpec(
            num_scalar_prefetch=1,
            grid=(n // tm,),
            in_specs=[
                pl.BlockSpec((tm, d), tok),
                pl.BlockSpec((tm, ATTN_W), tok_first),
                pl.BlockSpec((tm, ATTN_W), tok_second),
                pl.BlockSpec((tm, POOL_W), tok),
                pl.BlockSpec((tm, FOUR_W), tok_first),
                pl.BlockSpec((tm, FOUR_W), tok_second),
                pl.BlockSpec((d, d), const),
                pl.BlockSpec((1, 6, 1, d), lambda i, tmod: (tmod[i], 0, 0, 0)),
                pl.BlockSpec((1, d), const),
                pl.BlockSpec((d, LANES), const),
                pl.BlockSpec((1, LANES), const),
                pl.BlockSpec((tm, tm), const),
                pl.BlockSpec((LANES, LANES), const),
            ],
            out_specs=(pl.BlockSpec((tm, d), tok),
                       pl.BlockSpec((tm, d), tok),
                       pl.BlockSpec((8, tm), lane_tok),
                       pl.BlockSpec((tm, LANES), tok),
                       pl.BlockSpec((8, LANES), tok)),
        ),
        compiler_params=_cparams(("parallel",)),
        name="post_mixer_router",
    )(tile_mod, x, attn[0], attn[1], pool, four[0], four[1], wo_b, mods, n2.reshape(1, d), rw_pad, rb_pad,
      tri, low)


def _slot_copy_loops(i, off_ref, gbase_ref, nchunk_ref, start_copy):
    ratio = COPY_ROWS[0] // COPY_ROWS[1]

    def per_expert(e, carry):
        idx = i * N_EXPERTS + e
        local0 = off_ref[idx]
        global0 = gbase_ref[idx]
        n_small_units = nchunk_ref[idx]
        n_big = n_small_units // ratio
        rest0 = n_big * COPY_ROWS[0]

        def big(j, c):
            start_copy(pl.multiple_of(local0 + j * COPY_ROWS[0], SEG_ALIGN),
                       pl.multiple_of(global0 + j * COPY_ROWS[0], SEG_ALIGN), 0)
            return c

        def small(j, c):
            start_copy(pl.multiple_of(local0 + rest0 + j * COPY_ROWS[1], SEG_ALIGN),
                       pl.multiple_of(global0 + rest0 + j * COPY_ROWS[1], SEG_ALIGN), 1)
            return c

        lax.fori_loop(0, n_big, big, 0)
        lax.fori_loop(0, n_small_units - n_big * ratio, small, 0)
        return carry

    lax.fori_loop(0, N_EXPERTS, per_expert, 0)


def _wait_copies(copy, count_big, count_small):
    def wait_big(j, c):
        copy(0, 0, 0).wait()
        return c

    def wait_small(j, c):
        copy(0, 0, 1).wait()
        return c

    lax.fori_loop(0, count_big, wait_big, 0)
    lax.fori_loop(0, count_small, wait_small, 0)


def _dispatch_kernel(off_ref, gbase_ref, nchunk_ref, tbig_ref, tsmall_ref, npad_ref, padbase_ref,
                     nused_ref, h2_ref, pos_ref, xrows_ref, xs, zbuf, sem):
    i = pl.program_id(0)
    last = pl.num_programs(0) - 1
    slot = i % 2
    pos = pos_ref[...]
    rows = lax.broadcasted_iota(jnp.int32, (xs.shape[1], pos.shape[1]), 0)
    hit = rows == pos[0:1]
    for k in range(1, TOP_K):
        hit = hit | (rows == pos[k:k + 1])
    xs[slot] = _dot(hit.astype(BF16), h2_ref[...])

    def seg_copy(s, local, glob, size):
        rows = COPY_ROWS[size]
        return pltpu.make_async_copy(xs.at[s, pl.ds(local, rows)],
                                     xrows_ref.at[pl.ds(glob, rows)], sem.at[s, size])

    _slot_copy_loops(i, off_ref, gbase_ref, nchunk_ref,
                     lambda local, glob, size: seg_copy(slot, local, glob, size).start())

    @pl.when(i > 0)
    def _():
        prev = jnp.maximum(i - 1, 0)
        _wait_copies(functools.partial(seg_copy, 1 - slot), tbig_ref[prev], tsmall_ref[prev])

    @pl.when(i == last)
    def _():
        _wait_copies(functools.partial(seg_copy, slot), tbig_ref[i], tsmall_ref[i])
        zbuf[...] = jnp.zeros(zbuf.shape, zbuf.dtype)

        def zero_copy(dst, rows):
            return pltpu.make_async_copy(zbuf.at[pl.ds(0, rows)], xrows_ref.at[pl.ds(dst, rows)],
                                         sem.at[slot, 0])

        def per_expert(e, total):
            def per_chunk(j, c):
                zero_copy(pl.multiple_of(padbase_ref[e] + j * SEG_ALIGN, SEG_ALIGN), SEG_ALIGN).start()
                return c
            lax.fori_loop(0, npad_ref[e], per_chunk, 0)
            return total + npad_ref[e]

        total = lax.fori_loop(0, N_EXPERTS, per_expert, 0)

        def wait_zero(j, c):
            zero_copy(0, SEG_ALIGN).wait()
            return c

        lax.fori_loop(0, total, wait_zero, 0)

        bm = zbuf.shape[0]
        n_blocks = xrows_ref.shape[0] // bm

        def zero_block(b, c):
            zero_copy(pl.multiple_of(b * bm, bm), bm).start()
            return c

        def wait_block(b, c):
            zero_copy(0, bm).wait()
            return c

        lax.fori_loop(nused_ref[0], n_blocks, zero_block, 0)
        lax.fori_loop(nused_ref[0], n_blocks, wait_block, 0)


def moe_dispatch(h2, pos, tables, n_rows):
    n, d = h2.shape
    tm = TOKEN_TILE
    return pl.pallas_call(
        _dispatch_kernel,
        out_shape=jax.ShapeDtypeStruct((n_rows, d), F32),
        grid_spec=pltpu.PrefetchScalarGridSpec(
            num_scalar_prefetch=8,
            grid=(n // tm,),
            in_specs=[
                pl.BlockSpec((tm, d), lambda i, *_: (i, 0)),
                pl.BlockSpec((8, tm), lambda i, *_: (0, i)),
            ],
            out_specs=pl.BlockSpec(memory_space=pl.ANY),
            scratch_shapes=[pltpu.VMEM((2, TILE_SLOTS, d), F32),
                            pltpu.VMEM((EXPERT_BLOCK, d), F32),
                            pltpu.SemaphoreType.DMA((2, 2))],
        ),
        compiler_params=_cparams(("arbitrary",)),
        name="moe_dispatch",
    )(tables["off"], tables["gbase"], tables["nchunk"], tables["tbig"], tables["tsmall"],
      tables["npad"], tables["padbase"], tables["n_used"], h2, pos)


def _expert_kernel(be_ref, nused_ref, x_ref, wgu_ref, bgu_ref, wd_ref, bd_ref, y_ref, wgu_s, wd_s):
    b = pl.program_id(0)

    @pl.when((b == 0) | (be_ref[b] != be_ref[jnp.maximum(b - 1, 0)]))
    def _():
        wgu_s[...] = wgu_ref[0, 0].astype(BF16)
        wd_s[...] = wd_ref[0, 0].astype(BF16)

    @pl.when(b < nused_ref[0])
    def _():
        gu = _dot(x_ref[...].astype(BF16), wgu_s[...]) + bgu_ref[0]
        f = gu.shape[1] // 2
        glu = jnp.minimum(gu[:, :f], SWIGLU_LIMIT)
        lin = jnp.clip(gu[:, f:], -SWIGLU_LIMIT, SWIGLU_LIMIT)
        act = glu * jax.nn.sigmoid(SWIGLU_ALPHA * glu) * (lin + 1.0)
        y_ref[...] = _dot(act.astype(BF16), wd_s[...]) + bd_ref[0]

    @pl.when(b >= nused_ref[0])
    def _():
        y_ref[...] = jnp.zeros(y_ref.shape, y_ref.dtype)


def expert_ffn(block_expert, n_used, x_rows, layer, wgu, bgu, wd, bd):
    n_rows, d = x_rows.shape
    depth, e, _, f2 = wgu.shape
    bm = EXPERT_BLOCK
    return pl.pallas_call(
        _expert_kernel,
        out_shape=jax.ShapeDtypeStruct((n_rows, d), F32),
        grid_spec=pltpu.PrefetchScalarGridSpec(
            num_scalar_prefetch=2,
            grid=(n_rows // bm,),
            in_specs=[
                pl.BlockSpec((bm, d), lambda b, be, nu: (jnp.minimum(b, nu[0] - 1), 0)),
                pl.BlockSpec((1, 1, d, f2), lambda b, be, nu: (layer, be[b], 0, 0)),
                pl.BlockSpec((1, 1, f2), lambda b, be, nu: (layer * e + be[b], 0, 0)),
                pl.BlockSpec((1, 1, f2 // 2, d), lambda b, be, nu: (layer, be[b], 0, 0)),
                pl.BlockSpec((1, 1, d), lambda b, be, nu: (layer * e + be[b], 0, 0)),
            ],
            out_specs=pl.BlockSpec((bm, d), lambda b, be, nu: (b, 0)),
            scratch_shapes=[pltpu.VMEM((d, f2), BF16), pltpu.VMEM((f2 // 2, d), BF16)],
        ),
        compiler_params=_cparams(("arbitrary",)),
        name="expert_ffn",
    )(block_expert, n_used, x_rows, wgu, bgu.reshape(depth * e, 1, f2), wd, bd.reshape(depth * e, 1, d))


def _combine_kernel(tmod_ref, off_ref, gbase_ref, nchunk_ref, tbig_ref, tsmall_ref,
                    x_ref, gp_ref, mod_ref, yrows_ref, o_ref, ys, sem):
    del tmod_ref
    i = pl.program_id(0)
    last = pl.num_programs(0) - 1
    slot = i % 2

    def seg_copy(s, local, glob, size):
        rows = COPY_ROWS[size]
        return pltpu.make_async_copy(yrows_ref.at[pl.ds(glob, rows)],
                                     ys.at[s, pl.ds(local, rows)], sem.at[s, size])

    def fetch(tile, s):
        _slot_copy_loops(tile, off_ref, gbase_ref, nchunk_ref,
                         lambda local, glob, size: seg_copy(s, local, glob, size).start())

    @pl.when(i == 0)
    def _():
        ys[...] = jnp.zeros(ys.shape, ys.dtype)
        fetch(0, 0)

    @pl.when(i < last)
    def _():
        fetch(i + 1, 1 - slot)

    _wait_copies(functools.partial(seg_copy, slot), tbig_ref[i], tsmall_ref[i])

    gp = gp_ref[...]
    lanes = lax.broadcasted_iota(jnp.int32, (gp.shape[0], ys.shape[1]), 1).astype(F32)
    wmat = jnp.where(lanes == gp[:, TOP_K:TOP_K + 1], gp[:, 0:1], 0.0)
    for k in range(1, TOP_K):
        wmat = jnp.where(lanes == gp[:, TOP_K + k:TOP_K + k + 1], gp[:, k:k + 1], wmat)
    wh, wl = _split_bf16(wmat)
    yb = ys[slot].astype(BF16)
    o_ref[...] = x_ref[...] + mod_ref[0, 5] * (_dot(wh, yb) + _dot(wl, yb))


def moe_combine(x1, tile_mod, y_rows, gp, mods, tables):
    n, d = x1.shape
    tm = TOKEN_TILE
    return pl.pallas_call(
        _combine_kernel,
        out_shape=jax.ShapeDtypeStruct((n, d), F32),
        grid_spec=pltpu.PrefetchScalarGridSpec(
            num_scalar_prefetch=6,
            grid=(n // tm,),
            in_specs=[
                pl.BlockSpec((tm, d), lambda i, *_: (i, 0)),
                pl.BlockSpec((tm, LANES), lambda i, *_: (i, 0)),
                pl.BlockSpec((1, 6, 1, d), lambda i, tmod, *_: (tmod[i], 0, 0, 0)),
                pl.BlockSpec(memory_space=pl.ANY),
            ],
            out_specs=pl.BlockSpec((tm, d), lambda i, *_: (i, 0)),
            scratch_shapes=[pltpu.VMEM((2, TILE_SLOTS, d), F32),
                            pltpu.SemaphoreType.DMA((2, 2))],
        ),
        compiler_params=_cparams(("arbitrary",)),
        name="moe_combine",
    )(tile_mod, tables["off"], tables["gbase"], tables["nchunk"], tables["tbig"], tables["tsmall"],
      x1, gp, mods, y_rows)


def _rope_tables(t):
    quarter = HEAD_DIM // 4
    lane = jnp.arange(ATTN_W)
    inv_freq = ROPE_THETA ** (-(lane % quarter).astype(F32) / quarter)
    pos_t = jnp.arange(t)
    pos = jnp.where((lane % HEAD_DIM)[None, :] < HEAD_DIM // 2,
                    (pos_t // GRID_W)[:, None], (pos_t % GRID_W)[:, None]).astype(F32)
    ang = pos * inv_freq[None, :]
    sign = jnp.where(lane % (2 * quarter) < quarter, -1.0, 1.0).astype(F32)
    return jnp.cos(ang), jnp.sin(ang) * sign[None, :]


def _dft_tables(t):
    scale = 1.0 / math.sqrt(t)
    pos = jnp.arange(t, dtype=jnp.int32)

    def angles(freq, period):
        return ((freq[:, None] * pos[None, :]) % period).astype(F32) * (2.0 * math.pi / period)

    r = math.isqrt(t)
    if r * r != t:
        ang = angles(pos, t)
        return (jnp.cos(ang) * scale).astype(BF16), (jnp.sin(ang) * scale).astype(BF16)
    sub = jnp.arange(r, dtype=jnp.int32)
    hi, lo = angles(sub, r), angles(sub, t)
    ch, sh = jnp.cos(hi)[:, None, :] * scale, jnp.sin(hi)[:, None, :] * scale
    cl, sl = jnp.cos(lo)[None, :, :], jnp.sin(lo)[None, :, :]
    return ((ch * cl - sh * sl).reshape(t, t).astype(BF16),
            (sh * cl + ch * sl).reshape(t, t).astype(BF16))


def _block_diag(blocks):
    g, c, d = blocks.shape
    eye = jnp.eye(g, dtype=blocks.dtype)
    return (eye[:, None, :, None] * blocks[:, :, None, :]).reshape(g * c, g * d)


def _fourier_weights(fourier_w_l):
    c = jnp.arange(FOUR_GROUP, dtype=jnp.int32)
    ang = ((c[:, None] * c[None, :]) % FOUR_GROUP).astype(F32) * (2.0 * math.pi / FOUR_GROUP)
    scale = 1.0 / math.sqrt(FOUR_GROUP)
    hp = lax.Precision.HIGHEST
    a = jnp.einsum("ck,gkd->gcd", jnp.cos(ang) * scale, fourier_w_l, precision=hp)
    b = jnp.einsum("ck,gkd->gcd", jnp.sin(ang) * scale, fourier_w_l, precision=hp)
    return jnp.concatenate([_block_diag(a), _block_diag(b)], axis=1)


def kernel(x_prompt, x_sample, c, c_ctx, cache_k, cache_v, ada_w, ada_b, norm1_g, norm2_g, w_in,
           q_norm_g, k_norm_g, lambda_qk, subln_g, pool_w, pool_scale, fourier_w, w_out,
           router_w, router_b, w_gate_up, b_gate_up, w_down, b_down):
    bp, tp, d = x_prompt.shape
    bs, ts, _ = x_sample.shape
    depth = ada_w.shape[0]
    past = cache_k.shape[2]
    tm = TOKEN_TILE
    n_p, n_s = bp * tp, bs * ts
    n = n_p + n_s
    assert tp % tm == 0 and ts % tm == 0 and past % KV_CHUNK == 0 and bs + 1 <= MOD_ROWS

    tile_mod = np.concatenate([np.zeros(n_p // tm, np.int32),
                               1 + np.repeat(np.arange(bs, dtype=np.int32), ts // tm)])
    tile_t0 = np.concatenate([np.tile(np.arange(0, tp, tm, dtype=np.int32), bp),
                              np.tile(np.arange(0, ts, tm, dtype=np.int32), bs)])
    tile_len = np.concatenate([np.full(n_p // tm, tp, np.int32), np.full(n_s // tm, ts, np.int32)])
    tile_mod, tile_t0, tile_len = map(jnp.asarray, (tile_mod, tile_t0, tile_len))

    cond = jnp.zeros((MOD_ROWS, d), F32).at[0].set(c_ctx).at[1:1 + bs].set(c)
    mods_all = modulation_all(cond, ada_w, ada_b).reshape(depth, MOD_ROWS, 6, 1, d)

    cos_t, sin_t = _rope_tables(ts)
    dft_p = _dft_tables(tp)
    dft_s = _dft_tables(ts)
    gmat = _block_diag(jnp.ones((ATTN_W // HEAD_DIM, HEAD_DIM, HEAD_DIM), BF16))
    tri = (jnp.arange(tm)[:, None] < jnp.arange(tm)[None, :]).astype(BF16)
    tile_w = ATTN_W // HEAD_DIM

    low = (jnp.arange(LANES)[:, None] > jnp.arange(LANES)[None, :]).astype(BF16)
    n_tiles = n // tm
    bm = EXPERT_BLOCK
    max_rows = n * TOP_K + n_tiles * N_EXPERTS * (SEG_ALIGN - 1)
    n_blocks = -(-max_rows // bm) + N_EXPERTS
    n_rows = n_blocks * bm

    x = jnp.concatenate([x_prompt.reshape(n_p, d), x_sample.reshape(n_s, d)], axis=0)
    new_k, new_v = [], []
    for l in range(depth):
        lam_init = 0.8 - 0.6 * math.exp(-0.3 * l)
        mods = mods_all[l]
        lf = lambda_qk[l]
        lam = (jnp.exp(jnp.sum(lf[0] * lf[1])) - jnp.exp(jnp.sum(lf[2] * lf[3])) + lam_init).reshape(1)

        ab = _fourier_weights(fourier_w[l]).astype(BF16)
        proj, za, zb = pre_mixer(x, tile_mod, norm1_g[l], mods, w_in[l].astype(BF16), ab)

        qg = jnp.tile(q_norm_g[l], tile_w).reshape(1, ATTN_W)
        kg = jnp.tile(k_norm_g[l], tile_w).reshape(1, ATTN_W)

        qt_p, kk_p, vt_p, kn_p = qkv_prep(proj, 0, bp, tp, gmat, qg, kg, None, None,
                                          rope=False, emit_kn=True)
        attn_p = diff_attention(lam, qt_p, kk_p, vt_p, subln_g[l], lam_init)
        new_k.append(kn_p.reshape(bp, tp, N_HEADS, 2, HEAD_DIM))
        new_v.append(proj[:n_p, 2 * ATTN_W:3 * ATTN_W].reshape(bp, tp, N_HEADS, V_DIM))

        qt_s, kk_s, vt_s = qkv_prep(proj, n_p, bs, ts, gmat, qg, kg, cos_t, sin_t,
                                    rope=True, emit_kn=False)
        ck = cache_k[:, l].reshape(bs, past, ATTN_W).astype(BF16)
        cv = cache_v[:, l].reshape(bs, past // KV_CHUNK, KV_CHUNK, N_HEADS, V_DIM).astype(BF16)
        cv = jnp.concatenate([cv.transpose(0, 1, 3, 4, 2),
                              jnp.ones((bs, past // KV_CHUNK, N_HEADS, V_ONES_ROWS, KV_CHUNK), BF16)], axis=3)
        cv = cv.reshape(bs, past // KV_CHUNK, N_HEADS * V_AUG, KV_CHUNK)
        attn_s = diff_attention(lam, qt_s, kk_s, vt_s, subln_g[l], lam_init, ctx=(ck, cv))

        pool = multiscale_pool(proj, tile_t0, tile_len, _block_diag(pool_w[l]).astype(BF16),
                               pool_scale[l])

        four_p = fourier_mix(dft_p[0], dft_p[1],
                             za[:n_p].reshape(bp, tp, FOUR_W), zb[:n_p].reshape(bp, tp, FOUR_W))
        four_s = fourier_mix(dft_s[0], dft_s[1],
                             za[n_p:].reshape(bs, ts, FOUR_W), zb[n_p:].reshape(bs, ts, FOUR_W))
        four_p, four_s = four_p.reshape(n_p, FOUR_W), four_s.reshape(n_s, FOUR_W)

        rw_pad = jnp.zeros((d, LANES), F32).at[:, :N_EXPERTS].set(router_w[l])
        rb_pad = jnp.full((1, LANES), -1e30, F32).at[0, :N_EXPERTS].set(router_b[l])
        x1, h2, pos, gp, counts = post_mixer(
            x, tile_mod, (attn_p, attn_s), pool, (four_p, four_s), w_out[l].astype(BF16), mods,
            norm2_g[l], rw_pad, rb_pad, tri, low)

        cnt = counts.reshape(n_tiles, 8, LANES)[:, 0, :N_EXPERTS].astype(jnp.int32)
        seg = (cnt + SEG_ALIGN - 1) // SEG_ALIGN * SEG_ALIGN
        off = jnp.cumsum(seg, axis=1) - seg
        rows_e = jnp.sum(seg, axis=0)
        padded = (rows_e + bm - 1) // bm * bm
        pad_end = jnp.cumsum(padded)
        pad_start = pad_end - padded
        gbase = pad_start[None, :] + jnp.cumsum(seg, axis=0) - seg
        nchunk = seg // COPY_ROWS[1]
        copy_ratio = COPY_ROWS[0] // COPY_ROWS[1]
        n_used = (pad_end[-1] // bm).astype(jnp.int32).reshape(1)
        tables = {
            "off": off.reshape(-1), "gbase": gbase.reshape(-1), "nchunk": nchunk.reshape(-1),
            "tbig": jnp.sum(nchunk // copy_ratio, axis=1), "tsmall": jnp.sum(nchunk % copy_ratio, axis=1),
            "npad": (padded - rows_e) // SEG_ALIGN,
            "padbase": pad_start + rows_e, "n_used": n_used,
        }
        block_row0 = jnp.arange(n_blocks, dtype=jnp.int32) * bm
        block_expert = jnp.minimum(
            jnp.sum((pad_end[None, :] <= block_row0[:, None]).astype(jnp.int32), axis=1), N_EXPERTS - 1)

        x_rows = moe_dispatch(h2, pos, tables, n_rows)
        y_rows = expert_ffn(block_expert, n_used, x_rows, l, w_gate_up, b_gate_up, w_down, b_down)
        x = moe_combine(x1, tile_mod, y_rows, gp, mods, tables)

    y_prompt = x[:n_p].reshape(bp, tp, d)
    y_sample = x[n_p:].reshape(bs, ts, d)
    return y_prompt, y_sample, jnp.stack(new_k, axis=1), jnp.stack(new_v, axis=1)
```

```python
import functools
import math

import numpy as np
import jax
import jax.numpy as jnp
from jax import lax
from jax.experimental import pallas as pl
from jax.experimental.pallas import tpu as pltpu

F32 = jnp.float32
BF16 = jnp.bfloat16

GRID_W = 64
HEAD_DIM = 64
V_DIM = 2 * HEAD_DIM
N_HEADS = 4
ATTN_W = N_HEADS * V_DIM
POOL_WINDOWS = (2, 4, 8, 16)
POOL_GROUP = 64
POOL_W = 256
FOUR_GROUPS = 4
FOUR_GROUP = 64
FOUR_W = 256
N_EXPERTS = 32
TOP_K = 4
SWIGLU_LIMIT = 7.0
SWIGLU_ALPHA = 1.702
ROPE_THETA = 10000.0
RMS_EPS = 1e-6

LANES = 128
TOKEN_TILE = 256
POOL_HALO = 8
KV_CHUNK = 256
ATTN_STEP_CHUNKS = 2
ATTN_Q_TILE = 512
V_ONES_ROWS = 16
V_AUG = V_DIM + V_ONES_ROWS
EXPERT_BLOCK = 256
SEG_ALIGN = 8
TILE_SLOTS = TOKEN_TILE * TOP_K + N_EXPERTS * SEG_ALIGN
COPY_ROWS = (4 * SEG_ALIGN, SEG_ALIGN)
DFT_ROW_TILE = 512
MOD_ROWS = 16
VMEM_LIMIT = 56 * 1024 * 1024


def _dot(a, b):
    return jnp.dot(a, b, preferred_element_type=F32)


def _split_bf16(x):
    hi = x.astype(BF16)
    lo = (x - hi.astype(F32)).astype(BF16)
    return hi, lo


def _cparams(sem):
    return pltpu.CompilerParams(dimension_semantics=sem, vmem_limit_bytes=VMEM_LIMIT)


def _mod_kernel(cond_ref, w_ref, b_ref, o_ref):
    c = cond_ref[...]
    a = c * jax.nn.sigmoid(c)
    ah, al = _split_bf16(a)
    wh, wl = _split_bf16(w_ref[0])
    o_ref[0] = _dot(ah, wh) + _dot(ah, wl) + _dot(al, wh) + b_ref[0]


def modulation_all(cond, ada_w, ada_b):
    depth, d, d6 = ada_w.shape
    tn = d6 // 4
    return pl.pallas_call(
        _mod_kernel,
        out_shape=jax.ShapeDtypeStruct((depth, MOD_ROWS, d6), F32),
        grid=(depth, d6 // tn),
        in_specs=[
            pl.BlockSpec((MOD_ROWS, d), lambda l, j: (0, 0)),
            pl.BlockSpec((1, d, tn), lambda l, j: (l, 0, j)),
            pl.BlockSpec((1, 1, tn), lambda l, j: (l, 0, j)),
        ],
        out_specs=pl.BlockSpec((1, MOD_ROWS, tn), lambda l, j: (l, 0, j)),
        compiler_params=_cparams(("parallel", "parallel")),
        name="modulation",
    )(cond, ada_w, ada_b.reshape(depth, 1, d6))


def _rms_mod(x, g, scale, shift):
    ms = jnp.mean(x * x, axis=-1, keepdims=True)
    return x * lax.rsqrt(ms + RMS_EPS) * g * (1.0 + scale) + shift


def _pair_specs(block, first_tiles):
    return [pl.BlockSpec(block, lambda i, *_: (jnp.minimum(i, first_tiles - 1), 0)),
            pl.BlockSpec(block, lambda i, *_: (jnp.maximum(i - first_tiles, 0), 0))]


def _pick_group(first_ref, second_ref, first_tiles):
    return jnp.where(pl.program_id(0) < first_tiles, first_ref[...], second_ref[...])


def _pre_mixer_kernel(tmod_ref, xp_ref, xs_ref, g_ref, mod_ref, w_ref, ab_ref, proj_ref, za_ref, zb_ref,
                      *, first_tiles):
    del tmod_ref
    x = _pick_group(xp_ref, xs_ref, first_tiles)
    h = _rms_mod(x, g_ref[...], mod_ref[0, 1], mod_ref[0, 0])
    proj = _dot(h.astype(BF16), w_ref[...])
    nproj = proj_ref.shape[1]
    proj_ref[...] = proj[:, :nproj]
    z = _dot(proj[:, nproj:].astype(BF16), ab_ref[...])
    za_ref[...] = z[:, :FOUR_W].astype(BF16)
    zb_ref[...] = z[:, FOUR_W:].astype(BF16)


def pre_mixer(x, tile_mod, n1, mods, w_in_b, ab_b):
    d = x[0].shape[1]
    n = x[0].shape[0] + x[1].shape[0]
    in_w = w_in_b.shape[1]
    nproj = in_w - FOUR_W
    tm = TOKEN_TILE
    first_tiles = x[0].shape[0] // tm
    return pl.pallas_call(
        functools.partial(_pre_mixer_kernel, first_tiles=first_tiles),
        out_shape=(jax.ShapeDtypeStruct((n, nproj), F32),
                   jax.ShapeDtypeStruct((n, FOUR_W), BF16),
                   jax.ShapeDtypeStruct((n, FOUR_W), BF16)),
        grid_spec=pltpu.PrefetchScalarGridSpec(
            num_scalar_prefetch=1,
            grid=(n // tm,),
            in_specs=_pair_specs((tm, d), first_tiles) + [
                pl.BlockSpec((1, d), lambda i, tmod: (0, 0)),
                pl.BlockSpec((1, 6, 1, d), lambda i, tmod: (tmod[i], 0, 0, 0)),
                pl.BlockSpec((d, in_w), lambda i, tmod: (0, 0)),
                pl.BlockSpec((FOUR_W, 2 * FOUR_W), lambda i, tmod: (0, 0)),
            ],
            out_specs=(pl.BlockSpec((tm, nproj), lambda i, tmod: (i, 0)),
                       pl.BlockSpec((tm, FOUR_W), lambda i, tmod: (i, 0)),
                       pl.BlockSpec((tm, FOUR_W), lambda i, tmod: (i, 0))),
        ),
        compiler_params=_cparams(("parallel",)),
        name="pre_mixer",
    )(tile_mod, x[0], x[1], n1.reshape(1, d), mods, w_in_b, ab_b)


def _group_rms(x, gmat, g):
    ss = _dot((x * x).astype(BF16), gmat)
    return x * lax.rsqrt(ss * (1.0 / HEAD_DIM) + RMS_EPS) * g


def _rope(y, cos, sin_signed):
    w = y.shape[1]
    q = HEAD_DIM // 4
    lane = lax.broadcasted_iota(jnp.int32, y.shape, 1)
    partner = jnp.where(lane % (2 * q) < q, pltpu.roll(y, w - q, axis=1), pltpu.roll(y, q, axis=1))
    return y * cos + partner * sin_signed


def _qkv_kernel(q_ref, k_ref, v_ref, gmat_ref, qg_ref, kg_ref, *rest, rope, emit_kn):
    if rope:
        cos_ref, sin_ref, *outs = rest
    else:
        outs = rest
    qt_ref, kk_ref, vt_ref = outs[:3]
    gmat = gmat_ref[...]
    qn = _group_rms(q_ref[...], gmat, qg_ref[...])
    kn = _group_rms(k_ref[...], gmat, kg_ref[...])
    if emit_kn:
        outs[3][0] = kn
    if rope:
        cos = cos_ref[...]
        sin = sin_ref[...]
        qn = _rope(qn, cos, sin)
        kn = _rope(kn, cos, sin)
    qt_ref[0] = (qn * (HEAD_DIM ** -0.5 * math.log2(math.e))).T.astype(BF16)
    kk_ref[0] = kn.astype(BF16)
    vt = v_ref[...].T.astype(BF16)
    ones = jnp.ones((V_ONES_ROWS, vt.shape[1]), BF16)
    pieces = []
    for h in range(N_HEADS):
        pieces += [vt[h * V_DIM:(h + 1) * V_DIM], ones]
    vt_ref[0, 0] = jnp.concatenate(pieces, axis=0)


def qkv_prep(proj, row_off, b, t, gmat, qg, kg, cos, sin, *, rope, emit_kn):
    tm = TOKEN_TILE
    nt = t // tm
    off = row_off // tm
    w = ATTN_W
    row = lambda bi, j: (off + bi * nt + j)
    in_specs = [
        pl.BlockSpec((tm, w), lambda bi, j: (row(bi, j), 0)),
        pl.BlockSpec((tm, w), lambda bi, j: (row(bi, j), 1)),
        pl.BlockSpec((tm, w), lambda bi, j: (row(bi, j), 2)),
        pl.BlockSpec((w, w), lambda bi, j: (0, 0)),
        pl.BlockSpec((1, w), lambda bi, j: (0, 0)),
        pl.BlockSpec((1, w), lambda bi, j: (0, 0)),
    ]
    args = [proj, proj, proj, gmat, qg, kg]
    if rope:
        in_specs += [pl.BlockSpec((tm, w), lambda bi, j: (j, 0)),
                     pl.BlockSpec((tm, w), lambda bi, j: (j, 0))]
        args += [cos, sin]
    out_shape = [jax.ShapeDtypeStruct((b, w, t), BF16),
                 jax.ShapeDtypeStruct((b, t, w), BF16),
                 jax.ShapeDtypeStruct((b, nt, N_HEADS * V_AUG, tm), BF16)]
    out_specs = [pl.BlockSpec((1, w, tm), lambda bi, j: (bi, 0, j)),
                 pl.BlockSpec((1, tm, w), lambda bi, j: (bi, j, 0)),
                 pl.BlockSpec((1, 1, N_HEADS * V_AUG, tm), lambda bi, j: (bi, j, 0, 0))]
    if emit_kn:
        out_shape.append(jax.ShapeDtypeStruct((b, t, w), F32))
        out_specs.append(pl.BlockSpec((1, tm, w), lambda bi, j: (bi, j, 0)))
    return pl.pallas_call(
        functools.partial(_qkv_kernel, rope=rope, emit_kn=emit_kn),
        out_shape=tuple(out_shape),
        grid=(b, nt),
        in_specs=in_specs,
        out_specs=tuple(out_specs),
        compiler_params=_cparams(("parallel", "parallel")),
        name="qkv_prep_rope" if rope else "qkv_prep",
    )(*args)


def _attn_kernel(lam_ref, qt_ref, k_ref, vt_ref, *rest, n_own, n_ctx, sub, lam_init):
    if n_ctx:
        kc_ref, vc_ref, g_ref, o_ref = rest
    else:
        g_ref, o_ref = rest
    q = qt_ref[0]
    row = lax.broadcasted_iota(jnp.int32, q.shape, 0)
    zero = jnp.zeros_like(q)
    qmaps = (jnp.where(row < HEAD_DIM, q, zero), jnp.where(row >= HEAD_DIM, q, zero))
    tk = sub * KV_CHUNK
    steps = [(k_ref, vt_ref, c) for c in range(n_own)] + [(kc_ref, vc_ref, c) for c in range(n_ctx)]
    nk = len(steps)

    def scores(step):
        kr, _, c = steps[step]
        k = kr[0, c * tk:(c + 1) * tk, :]
        return [_dot(k, qm) for qm in qmaps]

    m_run, acc = [None, None], [None, None]
    s_next = scores(0)
    for c in range(nk):
        s_cur = s_next
        if c + 1 < nk:
            s_next = scores(c + 1)
        _, vr, cv = steps[c]
        for m in range(2):
            s = s_cur[m]
            m_blk = jnp.max(s, axis=0, keepdims=True)
            m_new = m_blk if c == 0 else jnp.maximum(m_run[m], m_blk)
            pb = jnp.exp2(s - m_new).astype(BF16)
            pv = _dot(vr[0, cv * sub], pb[:KV_CHUNK])
            for j in range(1, sub):
                pv = pv + _dot(vr[0, cv * sub + j], pb[j * KV_CHUNK:(j + 1) * KV_CHUNK])
            if c == 0:
                acc[m] = pv
            else:
                acc[m] = jnp.exp2(m_run[m] - m_new) * acc[m] + pv
            m_run[m] = m_new

    outs = [a[:V_DIM] / a[V_DIM:V_DIM + 1] for a in acc]
    o = outs[0] - lam_ref[0] * outs[1]
    ms = jnp.mean(o * o, axis=0, keepdims=True)
    o = o * lax.rsqrt(ms + RMS_EPS) * g_ref[...] * (1.0 - lam_init)
    o_ref[...] = o.T.astype(BF16)


def diff_attention(lam, qt, kk, vt, subln_g, lam_init, ctx=None):
    b, w, t = qt.shape
    tq = min(ATTN_Q_TILE, t)
    nq = t // tq
    hw = V_DIM
    kv = [(kk, vt)] + ([ctx] if ctx is not None else [])
    chunks = [k.shape[1] // KV_CHUNK for k, _ in kv]
    sub = ATTN_STEP_CHUNKS if all(c % ATTN_STEP_CHUNKS == 0 for c in chunks) else 1
    in_specs = [pl.BlockSpec((1, hw, tq), lambda bi, h, qi, lam: (bi, h, qi))]
    args = [qt]
    for (k, v), nc in zip(kv, chunks):
        in_specs += [pl.BlockSpec((1, k.shape[1], hw), lambda bi, h, qi, lam: (bi, 0, h)),
                     pl.BlockSpec((1, nc, V_AUG, KV_CHUNK), lambda bi, h, qi, lam: (bi, 0, h, 0))]
        args += [k, v]
    in_specs.append(pl.BlockSpec((hw, 1), lambda bi, h, qi, lam: (0, 0)))
    return pl.pallas_call(
        functools.partial(_attn_kernel, n_own=chunks[0] // sub,
                          n_ctx=chunks[1] // sub if ctx is not None else 0, sub=sub, lam_init=lam_init),
        out_shape=jax.ShapeDtypeStruct((b * t, w), BF16),
        grid_spec=pltpu.PrefetchScalarGridSpec(
            num_scalar_prefetch=1,
            grid=(b, N_HEADS, nq),
            in_specs=in_specs,
            out_specs=pl.BlockSpec((tq, hw), lambda bi, h, qi, lam: (bi * nq + qi, h)),
        ),
        compiler_params=_cparams(("parallel", "parallel", "parallel")),
        name="diff_attention",
    )(lam, *args, subln_g.reshape(hw, 1))


def _pool_kernel(t0_ref, tlen_ref, prev_ref, cur_ref, next_ref, pw_ref, ps_ref, o_ref):
    i = pl.program_id(0)
    t0 = t0_ref[i]
    tlen = tlen_ref[i]
    tm = cur_ref.shape[0]
    u = cur_ref[...]
    prev = jnp.where(t0 > 0, prev_ref[...], 0.0)
    nxt = jnp.where(t0 + tm < tlen, next_ref[...], 0.0)
    e = jnp.concatenate([prev, u, nxt], axis=0)
    n = e.shape[0]
    w2 = e + pltpu.roll(e, 1, axis=0)
    w4 = pltpu.roll(w2, 1, axis=0) + pltpu.roll(w2, n - 1, axis=0)
    w8 = pltpu.roll(w4, 2, axis=0) + pltpu.roll(w4, n - 2, axis=0)
    w16 = pltpu.roll(w8, 4, axis=0) + pltpu.roll(w8, n - 4, axis=0)
    sums = [w[POOL_HALO:POOL_HALO + tm] for w in (w2, w4, w8, w16)]
    lane = lax.broadcasted_iota(jnp.int32, u.shape, 1)
    grp = lane // POOL_GROUP
    tabs = t0 + lax.broadcasted_iota(jnp.int32, u.shape, 0)
    total = sums[3]
    half = jnp.full(u.shape, POOL_WINDOWS[3] // 2, jnp.int32)
    for gi in (2, 1, 0):
        total = jnp.where(grp == gi, sums[gi], total)
        half = jnp.where(grp == gi, POOL_WINDOWS[gi] // 2, half)
    cnt = jnp.minimum(tabs + half, tlen) - jnp.maximum(tabs - half, 0)
    d = total / cnt.astype(F32) - u
    o_ref[...] = (_dot(d.astype(BF16), pw_ref[...]) * ps_ref[...]).astype(BF16)


def multiscale_pool(proj, tile_t0, tile_len, pw_b, pool_scale):
    n = proj.shape[0]
    tm = TOKEN_TILE
    hb = tm // POOL_HALO
    col = (3 * ATTN_W) // POOL_W
    last = n // POOL_HALO - 1
    return pl.pallas_call(
        _pool_kernel,
        out_shape=jax.ShapeDtypeStruct((n, POOL_W), BF16),
        grid_spec=pltpu.PrefetchScalarGridSpec(
            num_scalar_prefetch=2,
            grid=(n // tm,),
            in_specs=[
                pl.BlockSpec((POOL_HALO, POOL_W), lambda i, a, b: (jnp.maximum(i * hb - 1, 0), col)),
                pl.BlockSpec((tm, POOL_W), lambda i, a, b: (i, col)),
                pl.BlockSpec((POOL_HALO, POOL_W), lambda i, a, b: (jnp.minimum((i + 1) * hb, last), col)),
                pl.BlockSpec((POOL_W, POOL_W), lambda i, a, b: (0, 0)),
                pl.BlockSpec((1, POOL_W), lambda i, a, b: (0, 0)),
            ],
            out_specs=pl.BlockSpec((tm, POOL_W), lambda i, a, b: (i, 0)),
        ),
        compiler_params=_cparams(("parallel",)),
        name="multiscale_pool",
    )(tile_t0, tile_len, proj, proj, proj, pw_b, pool_scale.reshape(1, POOL_W))


def _fourier_kernel(c_ref, s_ref, za_ref, zb_ref, o_ref):
    o_ref[0] = (_dot(c_ref[...], za_ref[0]) - _dot(s_ref[...], zb_ref[0])).astype(BF16)


def fourier_mix(cmat, smat, za, zb):
    b, t, w = za.shape
    tr = min(DFT_ROW_TILE, t)
    return pl.pallas_call(
        _fourier_kernel,
        out_shape=jax.ShapeDtypeStruct((b, t, w), BF16),
        grid=(t // tr, b),
        in_specs=[
            pl.BlockSpec((tr, t), lambda r, bi: (r, 0)),
            pl.BlockSpec((tr, t), lambda r, bi: (r, 0)),
            pl.BlockSpec((1, t, w), lambda r, bi: (bi, 0, 0)),
            pl.BlockSpec((1, t, w), lambda r, bi: (bi, 0, 0)),
        ],
        out_specs=pl.BlockSpec((1, tr, w), lambda r, bi: (bi, r, 0)),
        compiler_params=_cparams(("parallel", "parallel")),
        name="fourier_mix",
    )(cmat, smat, za, zb)


def _post_mixer_kernel(tmod_ref, xp_ref, xs_ref, ap_ref, as_ref, p_ref, fp_ref, fs_ref, wo_ref, mod_ref,
                       g2_ref, rw_ref, rb_ref, tri_ref, low_ref, x1_ref, h2_ref, pos_ref, gp_ref, cnt_ref,
                       *, first_tiles):
    del tmod_ref
    attn = _pick_group(ap_ref, as_ref, first_tiles)
    four = _pick_group(fp_ref, fs_ref, first_tiles)
    mixed = jnp.concatenate([attn, p_ref[...], four], axis=1)
    x1 = _pick_group(xp_ref, xs_ref, first_tiles) + mod_ref[0, 2] * _dot(mixed, wo_ref[...])
    x1_ref[...] = x1
    h2 = _rms_mod(x1, g2_ref[...], mod_ref[0, 4], mod_ref[0, 3])
    hh, hl = _split_bf16(h2)
    h2_ref[...] = hh

    rw = rw_ref[...]
    rwh, rwl = _split_bf16(rw)
    logits = _dot(hh, rwh) + _dot(hh, rwl) + _dot(hl, rwh) + rb_ref[...]
    lt = logits.T[:N_EXPERTS]
    eidx = lax.broadcasted_iota(jnp.int32, lt.shape, 0).astype(F32)
    work = lt
    vals, hots = [], []
    for _ in range(TOP_K):
        mx = jnp.max(work, axis=0, keepdims=True)
        ix = jnp.min(jnp.where(work == mx, eidx, float(N_EXPERTS)), axis=0, keepdims=True)
        hot = eidx == ix
        work = jnp.where(hot, -jnp.inf, work)
        vals.append(mx)
        hots.append(hot)
    exps = [jnp.exp(v - vals[0]) for v in vals]
    denom = exps[0] + exps[1] + exps[2] + exps[3]
    sel = (hots[0] | hots[1] | hots[2] | hots[3]).astype(F32)
    selb = sel.astype(BF16)
    prefix = _dot(selb, tri_ref[...])
    cnt_col = jnp.sum(sel, axis=1, keepdims=True)
    seg_col = jnp.ceil(cnt_col * (1.0 / SEG_ALIGN)) * SEG_ALIGN
    seg_start = _dot(low_ref[...], jnp.broadcast_to(seg_col, (N_EXPERTS, LANES)).astype(BF16))[:, 0:1]
    slot = prefix + seg_start
    tm = lt.shape[1]
    pos = [jnp.sum(jnp.where(h, slot, 0.0), axis=0, keepdims=True) for h in hots]
    pos_ref[...] = jnp.concatenate(pos + [jnp.zeros((8 - TOP_K, tm), F32)], axis=0).astype(jnp.int32)
    gp_t = jnp.concatenate([e / denom for e in exps] + pos
                           + [jnp.zeros((LANES - 2 * TOP_K, tm), F32)], axis=0)
    gp_ref[...] = gp_t.T
    sel_pad = jnp.concatenate([selb, jnp.zeros((LANES - N_EXPERTS, tm), BF16)], axis=0)
    cnt_ref[...] = lax.dot_general(jnp.ones((8, tm), BF16), sel_pad, (((1,), (1,)), ((), ())),
                                   preferred_element_type=F32)


def post_mixer(x, tile_mod, attn, pool, four, wo_b, mods, n2, rw_pad, rb_pad, tri, low):
    d = x[0].shape[1]
    n = x[0].shape[0] + x[1].shape[0]
    tm = TOKEN_TILE
    first_tiles = attn[0].shape[0] // tm
    tok = lambda i, tmod: (i, 0)
    const = lambda i, tmod: (0, 0)
    lane_tok = lambda i, tmod: (0, i)
    return pl.pallas_call(
        functools.partial(_post_mixer_kernel, first_tiles=first_tiles),
        out_shape=(jax.ShapeDtypeStruct((n, d), F32),
                   jax.ShapeDtypeStruct((n, d), BF16),
                   jax.ShapeDtypeStruct((8, n), jnp.int32),
                   jax.ShapeDtypeStruct((n, LANES), F32),
                   jax.ShapeDtypeStruct((n // tm * 8, LANES), F32)),
        grid_spec=pltpu.PrefetchScalarGridSpec(
            num_scalar_prefetch=1,
            grid=(n // tm,),
            in_specs=_pair_specs((tm, d), first_tiles) + _pair_specs((tm, ATTN_W), first_tiles) + [
                pl.BlockSpec((tm, POOL_W), tok),
            ] + _pair_specs((tm, FOUR_W), first_tiles) + [
                pl.BlockSpec((d, d), const),
                pl.BlockSpec((1, 6, 1, d), lambda i, tmod: (tmod[i], 0, 0, 0)),
                pl.BlockSpec((1, d), const),
                pl.BlockSpec((d, LANES), const),
                pl.BlockSpec((1, LANES), const),
                pl.BlockSpec((tm, tm), const),
                pl.BlockSpec((N_EXPERTS, N_EXPERTS), const),
            ],
            out_specs=(pl.BlockSpec((tm, d), tok),
                       pl.BlockSpec((tm, d), tok),
                       pl.BlockSpec((8, tm), lane_tok),
                       pl.BlockSpec((tm, LANES), tok),
                       pl.BlockSpec((8, LANES), tok)),
        ),
        compiler_params=_cparams(("parallel",)),
        name="post_mixer_router",
    )(tile_mod, x[0], x[1], attn[0], attn[1], pool, four[0], four[1], wo_b, mods, n2.reshape(1, d),
      rw_pad, rb_pad, tri, low)


def _slot_copy_loops(i, off_ref, gbase_ref, nchunk_ref, start_copy):
    ratio = COPY_ROWS[0] // COPY_ROWS[1]

    def per_expert(e, carry):
        idx = i * N_EXPERTS + e
        local0 = off_ref[idx]
        global0 = gbase_ref[idx]
        n_small_units = nchunk_ref[idx]
        n_big = n_small_units // ratio
        rest0 = n_big * COPY_ROWS[0]

        def big(j, c):
            start_copy(pl.multiple_of(local0 + j * COPY_ROWS[0], SEG_ALIGN),
                       pl.multiple_of(global0 + j * COPY_ROWS[0], SEG_ALIGN), 0)
            return c

        def small(j, c):
            start_copy(pl.multiple_of(local0 + rest0 + j * COPY_ROWS[1], SEG_ALIGN),
                       pl.multiple_of(global0 + rest0 + j * COPY_ROWS[1], SEG_ALIGN), 1)
            return c

        lax.fori_loop(0, n_big, big, 0)
        lax.fori_loop(0, n_small_units - n_big * ratio, small, 0)
        return carry

    lax.fori_loop(0, N_EXPERTS, per_expert, 0)


def _wait_copies(copy, count_big, count_small):
    def wait_big(j, c):
        copy(0, 0, 0).wait()
        return c

    def wait_small(j, c):
        copy(0, 0, 1).wait()
        return c

    lax.fori_loop(0, count_big, wait_big, 0)
    lax.fori_loop(0, count_small, wait_small, 0)


def _dispatch_kernel(off_ref, gbase_ref, nchunk_ref, tbig_ref, tsmall_ref, npad_ref, padbase_ref,
                     nused_ref, h2_ref, pos_ref, xrows_ref, xs, zbuf, sem):
    i = pl.program_id(0)
    last = pl.num_programs(0) - 1
    slot = i % 2
    pos = pos_ref[...]
    rows = lax.broadcasted_iota(jnp.int32, (xs.shape[1], pos.shape[1]), 0)
    hit = rows == pos[0:1]
    for k in range(1, TOP_K):
        hit = hit | (rows == pos[k:k + 1])
    xs[slot] = _dot(hit.astype(BF16), h2_ref[...])

    def seg_copy(s, local, glob, size):
        rows = COPY_ROWS[size]
        return pltpu.make_async_copy(xs.at[s, pl.ds(local, rows)],
                                     xrows_ref.at[pl.ds(glob, rows)], sem.at[s, size])

    _slot_copy_loops(i, off_ref, gbase_ref, nchunk_ref,
                     lambda local, glob, size: seg_copy(slot, local, glob, size).start())

    @pl.when(i > 0)
    def _():
        prev = jnp.maximum(i - 1, 0)
        _wait_copies(functools.partial(seg_copy, 1 - slot), tbig_ref[prev], tsmall_ref[prev])

    @pl.when(i == last)
    def _():
        _wait_copies(functools.partial(seg_copy, slot), tbig_ref[i], tsmall_ref[i])
        zbuf[...] = jnp.zeros(zbuf.shape, zbuf.dtype)

        def zero_copy(dst, rows):
            return pltpu.make_async_copy(zbuf.at[pl.ds(0, rows)], xrows_ref.at[pl.ds(dst, rows)],
                                         sem.at[slot, 0])

        def per_expert(e, total):
            def per_chunk(j, c):
                zero_copy(pl.multiple_of(padbase_ref[e] + j * SEG_ALIGN, SEG_ALIGN), SEG_ALIGN).start()
                return c
            lax.fori_loop(0, npad_ref[e], per_chunk, 0)
            return total + npad_ref[e]

        total = lax.fori_loop(0, N_EXPERTS, per_expert, 0)

        def wait_zero(j, c):
            zero_copy(0, SEG_ALIGN).wait()
            return c

        lax.fori_loop(0, total, wait_zero, 0)

        bm = zbuf.shape[0]
        n_blocks = xrows_ref.shape[0] // bm

        def zero_block(b, c):
            zero_copy(pl.multiple_of(b * bm, bm), bm).start()
            return c

        def wait_block(b, c):
            zero_copy(0, bm).wait()
            return c

        lax.fori_loop(nused_ref[0], n_blocks, zero_block, 0)
        lax.fori_loop(nused_ref[0], n_blocks, wait_block, 0)


def moe_dispatch(h2, pos, tables, n_rows):
    n, d = h2.shape
    tm = TOKEN_TILE
    return pl.pallas_call(
        _dispatch_kernel,
        out_shape=jax.ShapeDtypeStruct((n_rows, d), F32),
        grid_spec=pltpu.PrefetchScalarGridSpec(
            num_scalar_prefetch=8,
            grid=(n // tm,),
            in_specs=[
                pl.BlockSpec((tm, d), lambda i, *_: (i, 0)),
                pl.BlockSpec((8, tm), lambda i, *_: (0, i)),
            ],
            out_specs=pl.BlockSpec(memory_space=pl.ANY),
            scratch_shapes=[pltpu.VMEM((2, TILE_SLOTS, d), F32),
                            pltpu.VMEM((EXPERT_BLOCK, d), F32),
                            pltpu.SemaphoreType.DMA((2, 2))],
        ),
        compiler_params=_cparams(("arbitrary",)),
        name="moe_dispatch",
    )(tables["off"], tables["gbase"], tables["nchunk"], tables["tbig"], tables["tsmall"],
      tables["npad"], tables["padbase"], tables["n_used"], h2, pos)


def _expert_kernel(be_ref, nused_ref, x_ref, wgu_ref, bgu_ref, wd_ref, bd_ref, y_ref, wgu_s, wd_s):
    b = pl.program_id(0)

    @pl.when((b == 0) | (be_ref[b] != be_ref[jnp.maximum(b - 1, 0)]))
    def _():
        wgu_s[...] = wgu_ref[0, 0].astype(BF16)
        wd_s[...] = wd_ref[0, 0].astype(BF16)

    @pl.when(b < nused_ref[0])
    def _():
        gu = _dot(x_ref[...].astype(BF16), wgu_s[...]) + bgu_ref[0]
        f = gu.shape[1] // 2
        glu = jnp.minimum(gu[:, :f], SWIGLU_LIMIT)
        lin = jnp.clip(gu[:, f:], -SWIGLU_LIMIT, SWIGLU_LIMIT)
        act = glu * jax.nn.sigmoid(SWIGLU_ALPHA * glu) * (lin + 1.0)
        y_ref[...] = _dot(act.astype(BF16), wd_s[...]) + bd_ref[0]

    @pl.when(b >= nused_ref[0])
    def _():
        y_ref[...] = jnp.zeros(y_ref.shape, y_ref.dtype)


def expert_ffn(block_expert, n_used, x_rows, layer, wgu, bgu, wd, bd):
    n_rows, d = x_rows.shape
    depth, e, _, f2 = wgu.shape
    bm = EXPERT_BLOCK
    return pl.pallas_call(
        _expert_kernel,
        out_shape=jax.ShapeDtypeStruct((n_rows, d), F32),
        grid_spec=pltpu.PrefetchScalarGridSpec(
            num_scalar_prefetch=2,
            grid=(n_rows // bm,),
            in_specs=[
                pl.BlockSpec((bm, d), lambda b, be, nu: (jnp.minimum(b, nu[0] - 1), 0)),
                pl.BlockSpec((1, 1, d, f2), lambda b, be, nu: (layer, be[b], 0, 0)),
                pl.BlockSpec((1, 1, f2), lambda b, be, nu: (layer * e + be[b], 0, 0)),
                pl.BlockSpec((1, 1, f2 // 2, d), lambda b, be, nu: (layer, be[b], 0, 0)),
                pl.BlockSpec((1, 1, d), lambda b, be, nu: (layer * e + be[b], 0, 0)),
            ],
            out_specs=pl.BlockSpec((bm, d), lambda b, be, nu: (b, 0)),
            scratch_shapes=[pltpu.VMEM((d, f2), BF16), pltpu.VMEM((f2 // 2, d), BF16)],
        ),
        compiler_params=_cparams(("arbitrary",)),
        name="expert_ffn",
    )(block_expert, n_used, x_rows, wgu, bgu.reshape(depth * e, 1, f2), wd, bd.reshape(depth * e, 1, d))


def _combine_kernel(tmod_ref, off_ref, gbase_ref, nchunk_ref, tbig_ref, tsmall_ref,
                    x_ref, gp_ref, mod_ref, yrows_ref, op_ref, os_ref, ys, sem, *, first_tiles):
    del tmod_ref
    i = pl.program_id(0)
    last = pl.num_programs(0) - 1
    slot = i % 2

    def seg_copy(s, local, glob, size):
        rows = COPY_ROWS[size]
        return pltpu.make_async_copy(yrows_ref.at[pl.ds(glob, rows)],
                                     ys.at[s, pl.ds(local, rows)], sem.at[s, size])

    def fetch(tile, s):
        _slot_copy_loops(tile, off_ref, gbase_ref, nchunk_ref,
                         lambda local, glob, size: seg_copy(s, local, glob, size).start())

    @pl.when(i == 0)
    def _():
        ys[...] = jnp.zeros(ys.shape, ys.dtype)
        fetch(0, 0)

    @pl.when(i < last)
    def _():
        fetch(i + 1, 1 - slot)

    _wait_copies(functools.partial(seg_copy, slot), tbig_ref[i], tsmall_ref[i])

    gp = gp_ref[...]
    lanes = lax.broadcasted_iota(jnp.int32, (gp.shape[0], ys.shape[1]), 1).astype(F32)
    wmat = jnp.where(lanes == gp[:, TOP_K:TOP_K + 1], gp[:, 0:1], 0.0)
    for k in range(1, TOP_K):
        wmat = jnp.where(lanes == gp[:, TOP_K + k:TOP_K + k + 1], gp[:, k:k + 1], wmat)
    wh, wl = _split_bf16(wmat)
    yb = ys[slot].astype(BF16)
    x2 = x_ref[...] + mod_ref[0, 5] * (_dot(wh, yb) + _dot(wl, yb))

    @pl.when(i < first_tiles)
    def _():
        op_ref[...] = x2

    @pl.when(i >= first_tiles)
    def _():
        os_ref[...] = x2


def moe_combine(x1, tile_mod, y_rows, gp, mods, tables, first_rows):
    n, d = x1.shape
    tm = TOKEN_TILE
    first_tiles = first_rows // tm
    return pl.pallas_call(
        functools.partial(_combine_kernel, first_tiles=first_tiles),
        out_shape=(jax.ShapeDtypeStruct((first_rows, d), F32),
                   jax.ShapeDtypeStruct((n - first_rows, d), F32)),
        grid_spec=pltpu.PrefetchScalarGridSpec(
            num_scalar_prefetch=6,
            grid=(n // tm,),
            in_specs=[
                pl.BlockSpec((tm, d), lambda i, *_: (i, 0)),
                pl.BlockSpec((tm, LANES), lambda i, *_: (i, 0)),
                pl.BlockSpec((1, 6, 1, d), lambda i, tmod, *_: (tmod[i], 0, 0, 0)),
                pl.BlockSpec(memory_space=pl.ANY),
            ],
            out_specs=tuple(_pair_specs((tm, d), first_tiles)),
            scratch_shapes=[pltpu.VMEM((2, TILE_SLOTS, d), F32),
                            pltpu.SemaphoreType.DMA((2, 2))],
        ),
        compiler_params=_cparams(("arbitrary",)),
        name="moe_combine",
    )(tile_mod, tables["off"], tables["gbase"], tables["nchunk"], tables["tbig"], tables["tsmall"],
      x1, gp, mods, y_rows)


def _rope_tables(t):
    quarter = HEAD_DIM // 4
    lane = jnp.arange(ATTN_W)
    inv_freq = ROPE_THETA ** (-(lane % quarter).astype(F32) / quarter)
    pos_t = jnp.arange(t)
    pos = jnp.where((lane % HEAD_DIM)[None, :] < HEAD_DIM // 2,
                    (pos_t // GRID_W)[:, None], (pos_t % GRID_W)[:, None]).astype(F32)
    ang = pos * inv_freq[None, :]
    sign = jnp.where(lane % (2 * quarter) < quarter, -1.0, 1.0).astype(F32)
    return jnp.cos(ang), jnp.sin(ang) * sign[None, :]


def _dft_tables(t):
    scale = 1.0 / math.sqrt(t)
    pos = jnp.arange(t, dtype=jnp.int32)

    def angles(freq, period):
        return ((freq[:, None] * pos[None, :]) % period).astype(F32) * (2.0 * math.pi / period)

    r = math.isqrt(t)
    if r * r != t:
        ang = angles(pos, t)
        return (jnp.cos(ang) * scale).astype(BF16), (jnp.sin(ang) * scale).astype(BF16)
    sub = jnp.arange(r, dtype=jnp.int32)
    hi, lo = angles(sub, r), angles(sub, t)
    ch, sh = jnp.cos(hi)[:, None, :] * scale, jnp.sin(hi)[:, None, :] * scale
    cl, sl = jnp.cos(lo)[None, :, :], jnp.sin(lo)[None, :, :]
    return ((ch * cl - sh * sl).reshape(t, t).astype(BF16),
            (sh * cl + ch * sl).reshape(t, t).astype(BF16))


def _block_diag(blocks):
    g, c, d = blocks.shape
    eye = jnp.eye(g, dtype=blocks.dtype)
    return (eye[:, None, :, None] * blocks[:, :, None, :]).reshape(g * c, g * d)


def _fourier_weights(fourier_w_l):
    c = jnp.arange(FOUR_GROUP, dtype=jnp.int32)
    ang = ((c[:, None] * c[None, :]) % FOUR_GROUP).astype(F32) * (2.0 * math.pi / FOUR_GROUP)
    scale = 1.0 / math.sqrt(FOUR_GROUP)
    hp = lax.Precision.HIGHEST
    a = jnp.einsum("ck,gkd->gcd", jnp.cos(ang) * scale, fourier_w_l, precision=hp)
    b = jnp.einsum("ck,gkd->gcd", jnp.sin(ang) * scale, fourier_w_l, precision=hp)
    return jnp.concatenate([_block_diag(a), _block_diag(b)], axis=1)


def kernel(x_prompt, x_sample, c, c_ctx, cache_k, cache_v, ada_w, ada_b, norm1_g, norm2_g, w_in,
           q_norm_g, k_norm_g, lambda_qk, subln_g, pool_w, pool_scale, fourier_w, w_out,
           router_w, router_b, w_gate_up, b_gate_up, w_down, b_down):
    bp, tp, d = x_prompt.shape
    bs, ts, _ = x_sample.shape
    depth = ada_w.shape[0]
    past = cache_k.shape[2]
    tm = TOKEN_TILE
    n_p, n_s = bp * tp, bs * ts
    n = n_p + n_s
    assert tp % tm == 0 and ts % tm == 0 and past % KV_CHUNK == 0 and bs + 1 <= MOD_ROWS
    assert ts % min(ATTN_Q_TILE, ts) == 0 and tp % min(ATTN_Q_TILE, tp) == 0

    tile_mod = np.concatenate([np.zeros(n_p // tm, np.int32),
                               1 + np.repeat(np.arange(bs, dtype=np.int32), ts // tm)])
    tile_t0 = np.concatenate([np.tile(np.arange(0, tp, tm, dtype=np.int32), bp),
                              np.tile(np.arange(0, ts, tm, dtype=np.int32), bs)])
    tile_len = np.concatenate([np.full(n_p // tm, tp, np.int32), np.full(n_s // tm, ts, np.int32)])
    tile_mod, tile_t0, tile_len = map(jnp.asarray, (tile_mod, tile_t0, tile_len))

    cond = jnp.zeros((MOD_ROWS, d), F32).at[0].set(c_ctx).at[1:1 + bs].set(c)
    mods_all = modulation_all(cond, ada_w, ada_b).reshape(depth, MOD_ROWS, 6, 1, d)

    cos_t, sin_t = _rope_tables(ts)
    dft_p = _dft_tables(tp)
    dft_s = _dft_tables(ts)
    gmat = _block_diag(jnp.ones((ATTN_W // HEAD_DIM, HEAD_DIM, HEAD_DIM), BF16))
    tri = (jnp.arange(tm)[:, None] < jnp.arange(tm)[None, :]).astype(BF16)
    tile_w = ATTN_W // HEAD_DIM

    low = (jnp.arange(N_EXPERTS)[:, None] > jnp.arange(N_EXPERTS)[None, :]).astype(BF16)
    n_tiles = n // tm
    bm = EXPERT_BLOCK
    max_rows = n * TOP_K + n_tiles * N_EXPERTS * (SEG_ALIGN - 1)
    n_blocks = -(-max_rows // bm) + N_EXPERTS
    n_rows = n_blocks * bm

    x = (x_prompt.reshape(n_p, d), x_sample.reshape(n_s, d))
    new_k, new_v = [], []
    for l in range(depth):
        lam_init = 0.8 - 0.6 * math.exp(-0.3 * l)
        mods = mods_all[l]
        lf = lambda_qk[l]
        lam = (jnp.exp(jnp.sum(lf[0] * lf[1])) - jnp.exp(jnp.sum(lf[2] * lf[3])) + lam_init).reshape(1)

        ab = _fourier_weights(fourier_w[l]).astype(BF16)
        proj, za, zb = pre_mixer(x, tile_mod, norm1_g[l], mods, w_in[l].astype(BF16), ab)

        qg = jnp.tile(q_norm_g[l], tile_w).reshape(1, ATTN_W)
        kg = jnp.tile(k_norm_g[l], tile_w).reshape(1, ATTN_W)

        qt_p, kk_p, vt_p, kn_p = qkv_prep(proj, 0, bp, tp, gmat, qg, kg, None, None,
                                          rope=False, emit_kn=True)
        attn_p = diff_attention(lam, qt_p, kk_p, vt_p, subln_g[l], lam_init)
        new_k.append(kn_p.reshape(bp, tp, N_HEADS, 2, HEAD_DIM))
        new_v.append(proj[:n_p, 2 * ATTN_W:3 * ATTN_W].reshape(bp, tp, N_HEADS, V_DIM))

        qt_s, kk_s, vt_s = qkv_prep(proj, n_p, bs, ts, gmat, qg, kg, cos_t, sin_t,
                                    rope=True, emit_kn=False)
        ck = cache_k[:, l].reshape(bs, past, ATTN_W).astype(BF16)
        cv = cache_v[:, l].reshape(bs, past // KV_CHUNK, KV_CHUNK, N_HEADS, V_DIM).astype(BF16)
        cv = jnp.concatenate([cv.transpose(0, 1, 3, 4, 2),
                              jnp.ones((bs, past // KV_CHUNK, N_HEADS, V_ONES_ROWS, KV_CHUNK), BF16)], axis=3)
        cv = cv.reshape(bs, past // KV_CHUNK, N_HEADS * V_AUG, KV_CHUNK)
        attn_s = diff_attention(lam, qt_s, kk_s, vt_s, subln_g[l], lam_init, ctx=(ck, cv))

        pool = multiscale_pool(proj, tile_t0, tile_len, _block_diag(pool_w[l]).astype(BF16),
                               pool_scale[l])

        four_p = fourier_mix(dft_p[0], dft_p[1],
                             za[:n_p].reshape(bp, tp, FOUR_W), zb[:n_p].reshape(bp, tp, FOUR_W))
        four_s = fourier_mix(dft_s[0], dft_s[1],
                             za[n_p:].reshape(bs, ts, FOUR_W), zb[n_p:].reshape(bs, ts, FOUR_W))
        four_p, four_s = four_p.reshape(n_p, FOUR_W), four_s.reshape(n_s, FOUR_W)

        rw_pad = jnp.zeros((d, LANES), F32).at[:, :N_EXPERTS].set(router_w[l])
        rb_pad = jnp.full((1, LANES), -1e30, F32).at[0, :N_EXPERTS].set(router_b[l])
        x1, h2, pos, gp, counts = post_mixer(
            x, tile_mod, (attn_p, attn_s), pool, (four_p, four_s), w_out[l].astype(BF16), mods,
            norm2_g[l], rw_pad, rb_pad, tri, low)

        cnt = counts.reshape(n_tiles, 8, LANES)[:, 0, :N_EXPERTS].astype(jnp.int32)
        seg = (cnt + SEG_ALIGN - 1) // SEG_ALIGN * SEG_ALIGN
        off = jnp.cumsum(seg, axis=1) - seg
        rows_e = jnp.sum(seg, axis=0)
        padded = (rows_e + bm - 1) // bm * bm
        pad_end = jnp.cumsum(padded)
        pad_start = pad_end - padded
        gbase = pad_start[None, :] + jnp.cumsum(seg, axis=0) - seg
        nchunk = seg // COPY_ROWS[1]
        copy_ratio = COPY_ROWS[0] // COPY_ROWS[1]
        n_used = (pad_end[-1] // bm).astype(jnp.int32).reshape(1)
        tables = {
            "off": off.reshape(-1), "gbase": gbase.reshape(-1), "nchunk": nchunk.reshape(-1),
            "tbig": jnp.sum(nchunk // copy_ratio, axis=1), "tsmall": jnp.sum(nchunk % copy_ratio, axis=1),
            "npad": (padded - rows_e) // SEG_ALIGN,
            "padbase": pad_start + rows_e, "n_used": n_used,
        }
        block_row0 = jnp.arange(n_blocks, dtype=jnp.int32) * bm
        block_expert = jnp.minimum(
            jnp.sum((pad_end[None, :] <= block_row0[:, None]).astype(jnp.int32), axis=1), N_EXPERTS - 1)

        x_rows = moe_dispatch(h2, pos, tables, n_rows)
        y_rows = expert_ffn(block_expert, n_used, x_rows, l, w_gate_up, b_gate_up, w_down, b_down)
        x = moe_combine(x1, tile_mod, y_rows, gp, mods, tables, n_p)

    y_prompt = x[0].reshape(bp, tp, d)
    y_sample = x[1].reshape(bs, ts, d)
    return y_prompt, y_sample, jnp.stack(new_k, axis=1), jnp.stack(new_v, axis=1)
```

```python
import functools
import math

import numpy as np
import jax
import jax.numpy as jnp
from jax import lax
from jax.experimental import pallas as pl
from jax.experimental.pallas import tpu as pltpu

F32 = jnp.float32
BF16 = jnp.bfloat16

GRID_W = 64
HEAD_DIM = 64
V_DIM = 2 * HEAD_DIM
N_HEADS = 4
ATTN_W = N_HEADS * V_DIM
POOL_WINDOWS = (2, 4, 8, 16)
POOL_GROUP = 64
POOL_W = 256
FOUR_GROUPS = 4
FOUR_GROUP = 64
FOUR_W = 256
N_EXPERTS = 32
TOP_K = 4
SWIGLU_LIMIT = 7.0
SWIGLU_ALPHA = 1.702
ROPE_THETA = 10000.0
RMS_EPS = 1e-6

LANES = 128
TOKEN_TILE = 256
POOL_HALO = 8
KV_CHUNK = 256
ATTN_STEP_CHUNKS = 2
ATTN_Q_TILE = 512
V_ONES_ROWS = 16
V_AUG = V_DIM + V_ONES_ROWS
EXPERT_BLOCK = 256
SEG_ALIGN = 8
TILE_SLOTS = TOKEN_TILE * TOP_K + N_EXPERTS * SEG_ALIGN
COPY_ROWS = (4 * SEG_ALIGN, SEG_ALIGN)
DFT_ROW_TILE = 512
MOD_ROWS = 16
VMEM_LIMIT = 56 * 1024 * 1024


def _dot(a, b):
    return jnp.dot(a, b, preferred_element_type=F32)


def _split_bf16(x):
    hi = x.astype(BF16)
    lo = (x - hi.astype(F32)).astype(BF16)
    return hi, lo


def _cparams(sem):
    return pltpu.CompilerParams(dimension_semantics=sem, vmem_limit_bytes=VMEM_LIMIT)


def _mod_kernel(cond_ref, w_ref, b_ref, o_ref):
    c = cond_ref[...]
    a = c * jax.nn.sigmoid(c)
    ah, al = _split_bf16(a)
    wh, wl = _split_bf16(w_ref[0])
    o_ref[0] = _dot(ah, wh) + _dot(ah, wl) + _dot(al, wh) + b_ref[0]


def modulation_all(cond, ada_w, ada_b):
    depth, d, d6 = ada_w.shape
    tn = d6 // 4
    return pl.pallas_call(
        _mod_kernel,
        out_shape=jax.ShapeDtypeStruct((depth, MOD_ROWS, d6), F32),
        grid=(depth, d6 // tn),
        in_specs=[
            pl.BlockSpec((MOD_ROWS, d), lambda l, j: (0, 0)),
            pl.BlockSpec((1, d, tn), lambda l, j: (l, 0, j)),
            pl.BlockSpec((1, 1, tn), lambda l, j: (l, 0, j)),
        ],
        out_specs=pl.BlockSpec((1, MOD_ROWS, tn), lambda l, j: (l, 0, j)),
        compiler_params=_cparams(("parallel", "parallel")),
        name="modulation",
    )(cond, ada_w, ada_b.reshape(depth, 1, d6))


def _rms_mod(x, g, scale, shift):
    ms = jnp.mean(x * x, axis=-1, keepdims=True)
    return x * lax.rsqrt(ms + RMS_EPS) * g * (1.0 + scale) + shift


def _pair_specs(block, first_tiles):
    return [pl.BlockSpec(block, lambda i, *_: (jnp.minimum(i, first_tiles - 1), 0)),
            pl.BlockSpec(block, lambda i, *_: (jnp.maximum(i - first_tiles, 0), 0))]


def _pick_group(first_ref, second_ref, first_tiles):
    return jnp.where(pl.program_id(0) < first_tiles, first_ref[...], second_ref[...])


def _pre_mixer_kernel(tmod_ref, xp_ref, xs_ref, g_ref, mod_ref, w_ref, ab_ref, proj_ref, za_ref, zb_ref,
                      *, first_tiles):
    del tmod_ref
    x = _pick_group(xp_ref, xs_ref, first_tiles)
    h = _rms_mod(x, g_ref[...], mod_ref[0, 1], mod_ref[0, 0])
    proj = _dot(h.astype(BF16), w_ref[...])
    nproj = proj_ref.shape[1]
    proj_ref[...] = proj[:, :nproj]
    z = _dot(proj[:, nproj:].astype(BF16), ab_ref[...])
    za_ref[...] = z[:, :FOUR_W].astype(BF16)
    zb_ref[...] = z[:, FOUR_W:].astype(BF16)


def pre_mixer(x, tile_mod, n1, mods, w_in_b, ab_b):
    d = x[0].shape[1]
    n = x[0].shape[0] + x[1].shape[0]
    in_w = w_in_b.shape[1]
    nproj = in_w - FOUR_W
    tm = TOKEN_TILE
    first_tiles = x[0].shape[0] // tm
    return pl.pallas_call(
        functools.partial(_pre_mixer_kernel, first_tiles=first_tiles),
        out_shape=(jax.ShapeDtypeStruct((n, nproj), F32),
                   jax.ShapeDtypeStruct((n, FOUR_W), BF16),
                   jax.ShapeDtypeStruct((n, FOUR_W), BF16)),
        grid_spec=pltpu.PrefetchScalarGridSpec(
            num_scalar_prefetch=1,
            grid=(n // tm,),
            in_specs=_pair_specs((tm, d), first_tiles) + [
                pl.BlockSpec((1, d), lambda i, tmod: (0, 0)),
                pl.BlockSpec((1, 6, 1, d), lambda i, tmod: (tmod[i], 0, 0, 0)),
                pl.BlockSpec((d, in_w), lambda i, tmod: (0, 0)),
                pl.BlockSpec((FOUR_W, 2 * FOUR_W), lambda i, tmod: (0, 0)),
            ],
            out_specs=(pl.BlockSpec((tm, nproj), lambda i, tmod: (i, 0)),
                       pl.BlockSpec((tm, FOUR_W), lambda i, tmod: (i, 0)),
                       pl.BlockSpec((tm, FOUR_W), lambda i, tmod: (i, 0))),
        ),
        compiler_params=_cparams(("parallel",)),
        name="pre_mixer",
    )(tile_mod, x[0], x[1], n1.reshape(1, d), mods, w_in_b, ab_b)


def _group_rms(x, gmat, g):
    ss = _dot((x * x).astype(BF16), gmat)
    return x * lax.rsqrt(ss * (1.0 / HEAD_DIM) + RMS_EPS) * g


def _rope(y, cos, sin_signed):
    w = y.shape[1]
    q = HEAD_DIM // 4
    lane = lax.broadcasted_iota(jnp.int32, y.shape, 1)
    partner = jnp.where(lane % (2 * q) < q, pltpu.roll(y, w - q, axis=1), pltpu.roll(y, q, axis=1))
    return y * cos + partner * sin_signed


def _qkv_kernel(q_ref, k_ref, v_ref, gmat_ref, qg_ref, kg_ref, *rest, rope, emit_kn):
    if rope:
        cos_ref, sin_ref, *outs = rest
    else:
        outs = rest
    qt_ref, kk_ref, vt_ref = outs[:3]
    gmat = gmat_ref[...]
    qn = _group_rms(q_ref[...], gmat, qg_ref[...])
    kn = _group_rms(k_ref[...], gmat, kg_ref[...])
    if emit_kn:
        outs[3][0] = kn
    if rope:
        cos = cos_ref[...]
        sin = sin_ref[...]
        qn = _rope(qn, cos, sin)
        kn = _rope(kn, cos, sin)
    qt_ref[0] = (qn * (HEAD_DIM ** -0.5 * math.log2(math.e))).T.astype(BF16)
    kk_ref[0] = kn.astype(BF16)
    vt = v_ref[...].T.astype(BF16)
    ones = jnp.ones((V_ONES_ROWS, vt.shape[1]), BF16)
    pieces = []
    for h in range(N_HEADS):
        pieces += [vt[h * V_DIM:(h + 1) * V_DIM], ones]
    vt_ref[0, 0] = jnp.concatenate(pieces, axis=0)


def qkv_prep(proj, row_off, b, t, gmat, qg, kg, cos, sin, *, rope, emit_kn):
    tm = TOKEN_TILE
    nt = t // tm
    off = row_off // tm
    w = ATTN_W
    row = lambda bi, j: (off + bi * nt + j)
    in_specs = [
        pl.BlockSpec((tm, w), lambda bi, j: (row(bi, j), 0)),
        pl.BlockSpec((tm, w), lambda bi, j: (row(bi, j), 1)),
        pl.BlockSpec((tm, w), lambda bi, j: (row(bi, j), 2)),
        pl.BlockSpec((w, w), lambda bi, j: (0, 0)),
        pl.BlockSpec((1, w), lambda bi, j: (0, 0)),
        pl.BlockSpec((1, w), lambda bi, j: (0, 0)),
    ]
    args = [proj, proj, proj, gmat, qg, kg]
    if rope:
        in_specs += [pl.BlockSpec((tm, w), lambda bi, j: (j, 0)),
                     pl.BlockSpec((tm, w), lambda bi, j: (j, 0))]
        args += [cos, sin]
    out_shape = [jax.ShapeDtypeStruct((b, w, t), BF16),
                 jax.ShapeDtypeStruct((b, t, w), BF16),
                 jax.ShapeDtypeStruct((b, nt, N_HEADS * V_AUG, tm), BF16)]
    out_specs = [pl.BlockSpec((1, w, tm), lambda bi, j: (bi, 0, j)),
                 pl.BlockSpec((1, tm, w), lambda bi, j: (bi, j, 0)),
                 pl.BlockSpec((1, 1, N_HEADS * V_AUG, tm), lambda bi, j: (bi, j, 0, 0))]
    if emit_kn:
        out_shape.append(jax.ShapeDtypeStruct((b, t, w), F32))
        out_specs.append(pl.BlockSpec((1, tm, w), lambda bi, j: (bi, j, 0)))
    return pl.pallas_call(
        functools.partial(_qkv_kernel, rope=rope, emit_kn=emit_kn),
        out_shape=tuple(out_shape),
        grid=(b, nt),
        in_specs=in_specs,
        out_specs=tuple(out_specs),
        compiler_params=_cparams(("parallel", "parallel")),
        name="qkv_prep_rope" if rope else "qkv_prep",
    )(*args)


def _attn_kernel(lam_ref, qt_ref, k_ref, vt_ref, *rest, n_own, n_ctx, sub, lam_init):
    if n_ctx:
        kc_ref, vc_ref, g_ref, o_ref = rest
    else:
        g_ref, o_ref = rest
    q = qt_ref[0]
    row = lax.broadcasted_iota(jnp.int32, q.shape, 0)
    zero = jnp.zeros_like(q)
    qmaps = (jnp.where(row < HEAD_DIM, q, zero), jnp.where(row >= HEAD_DIM, q, zero))
    tk = sub * KV_CHUNK
    steps = [(k_ref, vt_ref, c) for c in range(n_own)] + [(kc_ref, vc_ref, c) for c in range(n_ctx)]
    nk = len(steps)

    def scores(step):
        kr, _, c = steps[step]
        k = kr[0, c * tk:(c + 1) * tk, :]
        return [_dot(k, qm) for qm in qmaps]

    m_run, acc = [None, None], [None, None]
    s_next = scores(0)
    for c in range(nk):
        s_cur = s_next
        if c + 1 < nk:
            s_next = scores(c + 1)
        _, vr, cv = steps[c]
        for m in range(2):
            s = s_cur[m]
            m_blk = jnp.max(s, axis=0, keepdims=True)
            m_new = m_blk if c == 0 else jnp.maximum(m_run[m], m_blk)
            pb = jnp.exp2(s - m_new).astype(BF16)
            pv = _dot(vr[0, cv * sub], pb[:KV_CHUNK])
            for j in range(1, sub):
                pv = pv + _dot(vr[0, cv * sub + j], pb[j * KV_CHUNK:(j + 1) * KV_CHUNK])
            if c == 0:
                acc[m] = pv
            else:
                acc[m] = jnp.exp2(m_run[m] - m_new) * acc[m] + pv
            m_run[m] = m_new

    outs = [a[:V_DIM] / a[V_DIM:V_DIM + 1] for a in acc]
    o = outs[0] - lam_ref[0] * outs[1]
    ms = jnp.mean(o * o, axis=0, keepdims=True)
    o = o * lax.rsqrt(ms + RMS_EPS) * g_ref[...] * (1.0 - lam_init)
    o_ref[...] = o.T.astype(BF16)


def diff_attention(lam, qt, kk, vt, subln_g, lam_init, ctx=None):
    b, w, t = qt.shape
    tq = min(ATTN_Q_TILE, t)
    nq = t // tq
    hw = V_DIM
    kv = [(kk, vt)] + ([ctx] if ctx is not None else [])
    chunks = [k.shape[1] // KV_CHUNK for k, _ in kv]
    sub = ATTN_STEP_CHUNKS if all(c % ATTN_STEP_CHUNKS == 0 for c in chunks) else 1
    in_specs = [pl.BlockSpec((1, hw, tq), lambda bi, h, qi, lam: (bi, h, qi))]
    args = [qt]
    for (k, v), nc in zip(kv, chunks):
        in_specs += [pl.BlockSpec((1, k.shape[1], hw), lambda bi, h, qi, lam: (bi, 0, h)),
                     pl.BlockSpec((1, nc, V_AUG, KV_CHUNK), lambda bi, h, qi, lam: (bi, 0, h, 0))]
        args += [k, v]
    in_specs.append(pl.BlockSpec((hw, 1), lambda bi, h, qi, lam: (0, 0)))
    return pl.pallas_call(
        functools.partial(_attn_kernel, n_own=chunks[0] // sub,
                          n_ctx=chunks[1] // sub if ctx is not None else 0, sub=sub, lam_init=lam_init),
        out_shape=jax.ShapeDtypeStruct((b * t, w), BF16),
        grid_spec=pltpu.PrefetchScalarGridSpec(
            num_scalar_prefetch=1,
            grid=(b, N_HEADS, nq),
            in_specs=in_specs,
            out_specs=pl.BlockSpec((tq, hw), lambda bi, h, qi, lam: (bi * nq + qi, h)),
        ),
        compiler_params=_cparams(("parallel", "parallel", "parallel")),
        name="diff_attention",
    )(lam, *args, subln_g.reshape(hw, 1))


def _pool_kernel(t0_ref, tlen_ref, prev_ref, cur_ref, next_ref, pw_ref, ps_ref, o_ref):
    i = pl.program_id(0)
    t0 = t0_ref[i]
    tlen = tlen_ref[i]
    tm = cur_ref.shape[0]
    u = cur_ref[...]
    prev = jnp.where(t0 > 0, prev_ref[...], 0.0)
    nxt = jnp.where(t0 + tm < tlen, next_ref[...], 0.0)
    e = jnp.concatenate([prev, u, nxt], axis=0)
    n = e.shape[0]
    w2 = e + pltpu.roll(e, 1, axis=0)
    w4 = pltpu.roll(w2, 1, axis=0) + pltpu.roll(w2, n - 1, axis=0)
    w8 = pltpu.roll(w4, 2, axis=0) + pltpu.roll(w4, n - 2, axis=0)
    w16 = pltpu.roll(w8, 4, axis=0) + pltpu.roll(w8, n - 4, axis=0)
    sums = [w[POOL_HALO:POOL_HALO + tm] for w in (w2, w4, w8, w16)]
    lane = lax.broadcasted_iota(jnp.int32, u.shape, 1)
    grp = lane // POOL_GROUP
    tabs = t0 + lax.broadcasted_iota(jnp.int32, u.shape, 0)
    total = sums[3]
    half = jnp.full(u.shape, POOL_WINDOWS[3] // 2, jnp.int32)
    for gi in (2, 1, 0):
        total = jnp.where(grp == gi, sums[gi], total)
        half = jnp.where(grp == gi, POOL_WINDOWS[gi] // 2, half)
    cnt = jnp.minimum(tabs + half, tlen) - jnp.maximum(tabs - half, 0)
    d = total / cnt.astype(F32) - u
    o_ref[...] = (_dot(d.astype(BF16), pw_ref[...]) * ps_ref[...]).astype(BF16)


def multiscale_pool(proj, tile_t0, tile_len, pw_b, pool_scale):
    n = proj.shape[0]
    tm = TOKEN_TILE
    hb = tm // POOL_HALO
    col = (3 * ATTN_W) // POOL_W
    last = n // POOL_HALO - 1
    return pl.pallas_call(
        _pool_kernel,
        out_shape=jax.ShapeDtypeStruct((n, POOL_W), BF16),
        grid_spec=pltpu.PrefetchScalarGridSpec(
            num_scalar_prefetch=2,
            grid=(n // tm,),
            in_specs=[
                pl.BlockSpec((POOL_HALO, POOL_W), lambda i, a, b: (jnp.maximum(i * hb - 1, 0), col)),
                pl.BlockSpec((tm, POOL_W), lambda i, a, b: (i, col)),
                pl.BlockSpec((POOL_HALO, POOL_W), lambda i, a, b: (jnp.minimum((i + 1) * hb, last), col)),
                pl.BlockSpec((POOL_W, POOL_W), lambda i, a, b: (0, 0)),
                pl.BlockSpec((1, POOL_W), lambda i, a, b: (0, 0)),
            ],
            out_specs=pl.BlockSpec((tm, POOL_W), lambda i, a, b: (i, 0)),
        ),
        compiler_params=_cparams(("parallel",)),
        name="multiscale_pool",
    )(tile_t0, tile_len, proj, proj, proj, pw_b, pool_scale.reshape(1, POOL_W))


def _fourier_kernel(c_ref, s_ref, za_ref, zb_ref, o_ref):
    o_ref[0] = (_dot(c_ref[...], za_ref[0]) - _dot(s_ref[...], zb_ref[0])).astype(BF16)


def fourier_mix(cmat, smat, za, zb):
    b, t, w = za.shape
    tr = min(DFT_ROW_TILE, t)
    return pl.pallas_call(
        _fourier_kernel,
        out_shape=jax.ShapeDtypeStruct((b, t, w), BF16),
        grid=(t // tr, b),
        in_specs=[
            pl.BlockSpec((tr, t), lambda r, bi: (r, 0)),
            pl.BlockSpec((tr, t), lambda r, bi: (r, 0)),
            pl.BlockSpec((1, t, w), lambda r, bi: (bi, 0, 0)),
            pl.BlockSpec((1, t, w), lambda r, bi: (bi, 0, 0)),
        ],
        out_specs=pl.BlockSpec((1, tr, w), lambda r, bi: (bi, r, 0)),
        compiler_params=_cparams(("parallel", "parallel")),
        name="fourier_mix",
    )(cmat, smat, za, zb)


def _post_mixer_kernel(tmod_ref, xp_ref, xs_ref, ap_ref, as_ref, p_ref, fp_ref, fs_ref, wo_ref, mod_ref,
                       g2_ref, rw_ref, rb_ref, tri_ref, low_ref, x1_ref, h2_ref, pos_ref, gp_ref, cnt_ref,
                       *, first_tiles):
    del tmod_ref
    attn = _pick_group(ap_ref, as_ref, first_tiles)
    four = _pick_group(fp_ref, fs_ref, first_tiles)
    mixed = jnp.concatenate([attn, p_ref[...], four], axis=1)
    x1 = _pick_group(xp_ref, xs_ref, first_tiles) + mod_ref[0, 2] * _dot(mixed, wo_ref[...])
    x1_ref[...] = x1
    h2 = _rms_mod(x1, g2_ref[...], mod_ref[0, 4], mod_ref[0, 3])
    hh, hl = _split_bf16(h2)
    h2_ref[...] = hh

    rw = rw_ref[...]
    rwh, rwl = _split_bf16(rw)
    logits = _dot(hh, rwh) + _dot(hh, rwl) + _dot(hl, rwh) + rb_ref[...]
    lt = logits.T[:N_EXPERTS]
    eidx = lax.broadcasted_iota(jnp.int32, lt.shape, 0).astype(F32)
    work = lt
    vals, hots = [], []
    for _ in range(TOP_K):
        mx = jnp.max(work, axis=0, keepdims=True)
        ix = jnp.min(jnp.where(work == mx, eidx, float(N_EXPERTS)), axis=0, keepdims=True)
        hot = eidx == ix
        work = jnp.where(hot, -jnp.inf, work)
        vals.append(mx)
        hots.append(hot)
    exps = [jnp.exp(v - vals[0]) for v in vals]
    denom = exps[0] + exps[1] + exps[2] + exps[3]
    sel = (hots[0] | hots[1] | hots[2] | hots[3]).astype(F32)
    selb = sel.astype(BF16)
    prefix = _dot(selb, tri_ref[...])
    cnt_col = jnp.sum(sel, axis=1, keepdims=True)
    seg_col = jnp.ceil(cnt_col * (1.0 / SEG_ALIGN)) * SEG_ALIGN
    seg_start = _dot(low_ref[...], jnp.broadcast_to(seg_col, (N_EXPERTS, LANES)).astype(BF16))[:, 0:1]
    slot = prefix + seg_start
    tm = lt.shape[1]
    pos = [jnp.sum(jnp.where(h, slot, 0.0), axis=0, keepdims=True) for h in hots]
    pos_ref[...] = jnp.concatenate(pos + [jnp.zeros((8 - TOP_K, tm), F32)], axis=0).astype(jnp.int32)
    gp_t = jnp.concatenate([e / denom for e in exps] + pos
                           + [jnp.zeros((LANES - 2 * TOP_K, tm), F32)], axis=0)
    gp_ref[...] = gp_t.T
    sel_pad = jnp.concatenate([selb, jnp.zeros((LANES - N_EXPERTS, tm), BF16)], axis=0)
    cnt_ref[...] = lax.dot_general(jnp.ones((8, tm), BF16), sel_pad, (((1,), (1,)), ((), ())),
                                   preferred_element_type=F32)


def post_mixer(x, tile_mod, attn, pool, four, wo_b, mods, n2, rw_pad, rb_pad, tri, low):
    d = x[0].shape[1]
    n = x[0].shape[0] + x[1].shape[0]
    tm = TOKEN_TILE
    first_tiles = attn[0].shape[0] // tm
    tok = lambda i, tmod: (i, 0)
    const = lambda i, tmod: (0, 0)
    lane_tok = lambda i, tmod: (0, i)
    return pl.pallas_call(
        functools.partial(_post_mixer_kernel, first_tiles=first_tiles),
        out_shape=(jax.ShapeDtypeStruct((n, d), F32),
                   jax.ShapeDtypeStruct((n, d), BF16),
                   jax.ShapeDtypeStruct((8, n), jnp.int32),
                   jax.ShapeDtypeStruct((n, LANES), F32),
                   jax.ShapeDtypeStruct((n // tm * 8, LANES), F32)),
        grid_spec=pltpu.PrefetchScalarGridSpec(
            num_scalar_prefetch=1,
            grid=(n // tm,),
            in_specs=_pair_specs((tm, d), first_tiles) + _pair_specs((tm, ATTN_W), first_tiles) + [
                pl.BlockSpec((tm, POOL_W), tok),
            ] + _pair_specs((tm, FOUR_W), first_tiles) + [
                pl.BlockSpec((d, d), const),
                pl.BlockSpec((1, 6, 1, d), lambda i, tmod: (tmod[i], 0, 0, 0)),
                pl.BlockSpec((1, d), const),
                pl.BlockSpec((d, LANES), const),
                pl.BlockSpec((1, LANES), const),
                pl.BlockSpec((tm, tm), const),
                pl.BlockSpec((N_EXPERTS, N_EXPERTS), const),
            ],
            out_specs=(pl.BlockSpec((tm, d), tok),
                       pl.BlockSpec((tm, d), tok),
                       pl.BlockSpec((8, tm), lane_tok),
                       pl.BlockSpec((tm, LANES), tok),
                       pl.BlockSpec((8, LANES), tok)),
        ),
        compiler_params=_cparams(("parallel",)),
        name="post_mixer_router",
    )(tile_mod, x[0], x[1], attn[0], attn[1], pool, four[0], four[1], wo_b, mods, n2.reshape(1, d),
      rw_pad, rb_pad, tri, low)


def _slot_copy_loops(i, off_ref, gbase_ref, nchunk_ref, start_copy):
    ratio = COPY_ROWS[0] // COPY_ROWS[1]

    def per_expert(e, carry):
        idx = i * N_EXPERTS + e
        local0 = off_ref[idx]
        global0 = gbase_ref[idx]
        n_small_units = nchunk_ref[idx]
        n_big = n_small_units // ratio
        rest0 = n_big * COPY_ROWS[0]

        def big(j, c):
            start_copy(pl.multiple_of(local0 + j * COPY_ROWS[0], SEG_ALIGN),
                       pl.multiple_of(global0 + j * COPY_ROWS[0], SEG_ALIGN), 0)
            return c

        def small(j, c):
            start_copy(pl.multiple_of(local0 + rest0 + j * COPY_ROWS[1], SEG_ALIGN),
                       pl.multiple_of(global0 + rest0 + j * COPY_ROWS[1], SEG_ALIGN), 1)
            return c

        lax.fori_loop(0, n_big, big, 0)
        lax.fori_loop(0, n_small_units - n_big * ratio, small, 0)
        return carry

    lax.fori_loop(0, N_EXPERTS, per_expert, 0)


def _wait_copies(copy, count_big, count_small):
    def wait_big(j, c):
        copy(0, 0, 0).wait()
        return c

    def wait_small(j, c):
        copy(0, 0, 1).wait()
        return c

    lax.fori_loop(0, count_big, wait_big, 0)
    lax.fori_loop(0, count_small, wait_small, 0)


def _dispatch_kernel(off_ref, gbase_ref, nchunk_ref, tbig_ref, tsmall_ref, npad_ref, padbase_ref,
                     nused_ref, h2_ref, pos_ref, xrows_ref, xs, zbuf, sem):
    i = pl.program_id(0)
    last = pl.num_programs(0) - 1
    slot = i % 2
    pos = pos_ref[...]
    rows = lax.broadcasted_iota(jnp.int32, (xs.shape[1], pos.shape[1]), 0)
    hit = rows == pos[0:1]
    for k in range(1, TOP_K):
        hit = hit | (rows == pos[k:k + 1])
    xs[slot] = _dot(hit.astype(BF16), h2_ref[...])

    def seg_copy(s, local, glob, size):
        rows = COPY_ROWS[size]
        return pltpu.make_async_copy(xs.at[s, pl.ds(local, rows)],
                                     xrows_ref.at[pl.ds(glob, rows)], sem.at[s, size])

    _slot_copy_loops(i, off_ref, gbase_ref, nchunk_ref,
                     lambda local, glob, size: seg_copy(slot, local, glob, size).start())

    @pl.when(i > 0)
    def _():
        prev = jnp.maximum(i - 1, 0)
        _wait_copies(functools.partial(seg_copy, 1 - slot), tbig_ref[prev], tsmall_ref[prev])

    @pl.when(i == last)
    def _():
        _wait_copies(functools.partial(seg_copy, slot), tbig_ref[i], tsmall_ref[i])
        zbuf[...] = jnp.zeros(zbuf.shape, zbuf.dtype)

        def zero_copy(dst, rows):
            return pltpu.make_async_copy(zbuf.at[pl.ds(0, rows)], xrows_ref.at[pl.ds(dst, rows)],
                                         sem.at[slot, 0])

        def per_expert(e, total):
            def per_chunk(j, c):
                zero_copy(pl.multiple_of(padbase_ref[e] + j * SEG_ALIGN, SEG_ALIGN), SEG_ALIGN).start()
                return c
            lax.fori_loop(0, npad_ref[e], per_chunk, 0)
            return total + npad_ref[e]

        total = lax.fori_loop(0, N_EXPERTS, per_expert, 0)

        def wait_zero(j, c):
            zero_copy(0, SEG_ALIGN).wait()
            return c

        lax.fori_loop(0, total, wait_zero, 0)

        bm = zbuf.shape[0]
        n_blocks = xrows_ref.shape[0] // bm

        def zero_block(b, c):
            zero_copy(pl.multiple_of(b * bm, bm), bm).start()
            return c

        def wait_block(b, c):
            zero_copy(0, bm).wait()
            return c

        lax.fori_loop(nused_ref[0], n_blocks, zero_block, 0)
        lax.fori_loop(nused_ref[0], n_blocks, wait_block, 0)


def moe_dispatch(h2, pos, tables, n_rows):
    n, d = h2.shape
    tm = TOKEN_TILE
    return pl.pallas_call(
        _dispatch_kernel,
        out_shape=jax.ShapeDtypeStruct((n_rows, d), F32),
        grid_spec=pltpu.PrefetchScalarGridSpec(
            num_scalar_prefetch=8,
            grid=(n // tm,),
            in_specs=[
                pl.BlockSpec((tm, d), lambda i, *_: (i, 0)),
                pl.BlockSpec((8, tm), lambda i, *_: (0, i)),
            ],
            out_specs=pl.BlockSpec(memory_space=pl.ANY),
            scratch_shapes=[pltpu.VMEM((2, TILE_SLOTS, d), F32),
                            pltpu.VMEM((EXPERT_BLOCK, d), F32),
                            pltpu.SemaphoreType.DMA((2, 2))],
        ),
        compiler_params=_cparams(("arbitrary",)),
        name="moe_dispatch",
    )(tables["off"], tables["gbase"], tables["nchunk"], tables["tbig"], tables["tsmall"],
      tables["npad"], tables["padbase"], tables["n_used"], h2, pos)


def _expert_kernel(be_ref, nxt_ref, nused_ref, x_ref, wgu_ref, bgu_ref, wd_ref, bd_ref, y_ref,
                   stage_gu, stage_d, wgu_s, wd_s, sem, *, layer):
    b = pl.program_id(0)
    n_used = nused_ref[0]

    def weight_copies(e):
        return (pltpu.make_async_copy(wgu_ref.at[layer, e], stage_gu, sem.at[0]),
                pltpu.make_async_copy(wd_ref.at[layer, e], stage_d, sem.at[1]))

    first = ((b == 0) | (be_ref[b] != be_ref[jnp.maximum(b - 1, 0)])) & (b < n_used)

    @pl.when(first)
    def _():
        @pl.when(b == 0)
        def _():
            for c in weight_copies(be_ref[0]):
                c.start()

        for c in weight_copies(be_ref[b]):
            c.wait()
        wgu_s[...] = stage_gu[...].astype(BF16)
        wd_s[...] = stage_d[...].astype(BF16)
        nb = nxt_ref[b]

        @pl.when(nb < n_used)
        def _():
            for c in weight_copies(be_ref[jnp.minimum(nb, pl.num_programs(0) - 1)]):
                c.start()

    @pl.when(b < nused_ref[0])
    def _():
        gu = _dot(x_ref[...].astype(BF16), wgu_s[...]) + bgu_ref[0]
        f = gu.shape[1] // 2
        glu = jnp.minimum(gu[:, :f], SWIGLU_LIMIT)
        lin = jnp.clip(gu[:, f:], -SWIGLU_LIMIT, SWIGLU_LIMIT)
        act = glu * jax.nn.sigmoid(SWIGLU_ALPHA * glu) * (lin + 1.0)
        y_ref[...] = _dot(act.astype(BF16), wd_s[...]) + bd_ref[0]

    @pl.when(b >= nused_ref[0])
    def _():
        y_ref[...] = jnp.zeros(y_ref.shape, y_ref.dtype)


def expert_ffn(block_expert, next_first, n_used, x_rows, layer, wgu, bgu, wd, bd):
    n_rows, d = x_rows.shape
    depth, e, _, f2 = wgu.shape
    bm = EXPERT_BLOCK
    return pl.pallas_call(
        functools.partial(_expert_kernel, layer=layer),
        out_shape=jax.ShapeDtypeStruct((n_rows, d), F32),
        grid_spec=pltpu.PrefetchScalarGridSpec(
            num_scalar_prefetch=3,
            grid=(n_rows // bm,),
            in_specs=[
                pl.BlockSpec((bm, d), lambda b, be, nx, nu: (jnp.minimum(b, nu[0] - 1), 0)),
                pl.BlockSpec(memory_space=pl.ANY),
                pl.BlockSpec((1, 1, f2), lambda b, be, nx, nu: (layer * e + be[b], 0, 0)),
                pl.BlockSpec(memory_space=pl.ANY),
                pl.BlockSpec((1, 1, d), lambda b, be, nx, nu: (layer * e + be[b], 0, 0)),
            ],
            out_specs=pl.BlockSpec((bm, d), lambda b, be, nx, nu: (b, 0)),
            scratch_shapes=[pltpu.VMEM((d, f2), F32), pltpu.VMEM((f2 // 2, d), F32),
                            pltpu.VMEM((d, f2), BF16), pltpu.VMEM((f2 // 2, d), BF16),
                            pltpu.SemaphoreType.DMA((2,))],
        ),
        compiler_params=_cparams(("arbitrary",)),
        name="expert_ffn",
    )(block_expert, next_first, n_used, x_rows, wgu, bgu.reshape(depth * e, 1, f2), wd,
      bd.reshape(depth * e, 1, d))


def _combine_kernel(tmod_ref, off_ref, gbase_ref, nchunk_ref, tbig_ref, tsmall_ref,
                    x_ref, gp_ref, mod_ref, yrows_ref, op_ref, os_ref, ys, sem, *, first_tiles):
    del tmod_ref
    i = pl.program_id(0)
    last = pl.num_programs(0) - 1
    slot = i % 2

    def seg_copy(s, local, glob, size):
        rows = COPY_ROWS[size]
        return pltpu.make_async_copy(yrows_ref.at[pl.ds(glob, rows)],
                                     ys.at[s, pl.ds(local, rows)], sem.at[s, size])

    def fetch(tile, s):
        _slot_copy_loops(tile, off_ref, gbase_ref, nchunk_ref,
                         lambda local, glob, size: seg_copy(s, local, glob, size).start())

    @pl.when(i == 0)
    def _():
        ys[...] = jnp.zeros(ys.shape, ys.dtype)
        fetch(0, 0)

    @pl.when(i < last)
    def _():
        fetch(i + 1, 1 - slot)

    _wait_copies(functools.partial(seg_copy, slot), tbig_ref[i], tsmall_ref[i])

    gp = gp_ref[...]
    lanes = lax.broadcasted_iota(jnp.int32, (gp.shape[0], ys.shape[1]), 1).astype(F32)
    wmat = jnp.where(lanes == gp[:, TOP_K:TOP_K + 1], gp[:, 0:1], 0.0)
    for k in range(1, TOP_K):
        wmat = jnp.where(lanes == gp[:, TOP_K + k:TOP_K + k + 1], gp[:, k:k + 1], wmat)
    x2 = x_ref[...] + mod_ref[0, 5] * _dot(wmat.astype(BF16), ys[slot].astype(BF16))

    @pl.when(i < first_tiles)
    def _():
        op_ref[...] = x2

    @pl.when(i >= first_tiles)
    def _():
        os_ref[...] = x2


def moe_combine(x1, tile_mod, y_rows, gp, mods, tables, first_rows):
    n, d = x1.shape
    tm = TOKEN_TILE
    first_tiles = first_rows // tm
    return pl.pallas_call(
        functools.partial(_combine_kernel, first_tiles=first_tiles),
        out_shape=(jax.ShapeDtypeStruct((first_rows, d), F32),
                   jax.ShapeDtypeStruct((n - first_rows, d), F32)),
        grid_spec=pltpu.PrefetchScalarGridSpec(
            num_scalar_prefetch=6,
            grid=(n // tm,),
            in_specs=[
                pl.BlockSpec((tm, d), lambda i, *_: (i, 0)),
                pl.BlockSpec((tm, LANES), lambda i, *_: (i, 0)),
                pl.BlockSpec((1, 6, 1, d), lambda i, tmod, *_: (tmod[i], 0, 0, 0)),
                pl.BlockSpec(memory_space=pl.ANY),
            ],
            out_specs=tuple(_pair_specs((tm, d), first_tiles)),
            scratch_shapes=[pltpu.VMEM((2, TILE_SLOTS, d), F32),
                            pltpu.SemaphoreType.DMA((2, 2))],
        ),
        compiler_params=_cparams(("arbitrary",)),
        name="moe_combine",
    )(tile_mod, tables["off"], tables["gbase"], tables["nchunk"], tables["tbig"], tables["tsmall"],
      x1, gp, mods, y_rows)


def _rope_tables(t):
    quarter = HEAD_DIM // 4
    lane = jnp.arange(ATTN_W)
    inv_freq = ROPE_THETA ** (-(lane % quarter).astype(F32) / quarter)
    pos_t = jnp.arange(t)
    pos = jnp.where((lane % HEAD_DIM)[None, :] < HEAD_DIM // 2,
                    (pos_t // GRID_W)[:, None], (pos_t % GRID_W)[:, None]).astype(F32)
    ang = pos * inv_freq[None, :]
    sign = jnp.where(lane % (2 * quarter) < quarter, -1.0, 1.0).astype(F32)
    return jnp.cos(ang), jnp.sin(ang) * sign[None, :]


def _dft_tables(t):
    scale = 1.0 / math.sqrt(t)
    pos = jnp.arange(t, dtype=jnp.int32)

    def angles(freq, period):
        return ((freq[:, None] * pos[None, :]) % period).astype(F32) * (2.0 * math.pi / period)

    r = math.isqrt(t)
    if r * r != t:
        ang = angles(pos, t)
        return (jnp.cos(ang) * scale).astype(BF16), (jnp.sin(ang) * scale).astype(BF16)
    sub = jnp.arange(r, dtype=jnp.int32)
    hi, lo = angles(sub, r), angles(sub, t)
    ch, sh = jnp.cos(hi)[:, None, :] * scale, jnp.sin(hi)[:, None, :] * scale
    cl, sl = jnp.cos(lo)[None, :, :], jnp.sin(lo)[None, :, :]
    return ((ch * cl - sh * sl).reshape(t, t).astype(BF16),
            (sh * cl + ch * sl).reshape(t, t).astype(BF16))


def _block_diag(blocks):
    g, c, d = blocks.shape
    eye = jnp.eye(g, dtype=blocks.dtype)
    return (eye[:, None, :, None] * blocks[:, :, None, :]).reshape(g * c, g * d)


def _fourier_weights(fourier_w_l):
    c = jnp.arange(FOUR_GROUP, dtype=jnp.int32)
    ang = ((c[:, None] * c[None, :]) % FOUR_GROUP).astype(F32) * (2.0 * math.pi / FOUR_GROUP)
    scale = 1.0 / math.sqrt(FOUR_GROUP)
    hp = lax.Precision.HIGHEST
    a = jnp.einsum("ck,gkd->gcd", jnp.cos(ang) * scale, fourier_w_l, precision=hp)
    b = jnp.einsum("ck,gkd->gcd", jnp.sin(ang) * scale, fourier_w_l, precision=hp)
    return jnp.concatenate([_block_diag(a), _block_diag(b)], axis=1)


def kernel(x_prompt, x_sample, c, c_ctx, cache_k, cache_v, ada_w, ada_b, norm1_g, norm2_g, w_in,
           q_norm_g, k_norm_g, lambda_qk, subln_g, pool_w, pool_scale, fourier_w, w_out,
           router_w, router_b, w_gate_up, b_gate_up, w_down, b_down):
    bp, tp, d = x_prompt.shape
    bs, ts, _ = x_sample.shape
    depth = ada_w.shape[0]
    past = cache_k.shape[2]
    tm = TOKEN_TILE
    n_p, n_s = bp * tp, bs * ts
    n = n_p + n_s
    assert tp % tm == 0 and ts % tm == 0 and past % KV_CHUNK == 0 and bs + 1 <= MOD_ROWS
    assert ts % min(ATTN_Q_TILE, ts) == 0 and tp % min(ATTN_Q_TILE, tp) == 0

    tile_mod = np.concatenate([np.zeros(n_p // tm, np.int32),
                               1 + np.repeat(np.arange(bs, dtype=np.int32), ts // tm)])
    tile_t0 = np.concatenate([np.tile(np.arange(0, tp, tm, dtype=np.int32), bp),
                              np.tile(np.arange(0, ts, tm, dtype=np.int32), bs)])
    tile_len = np.concatenate([np.full(n_p // tm, tp, np.int32), np.full(n_s // tm, ts, np.int32)])
    tile_mod, tile_t0, tile_len = map(jnp.asarray, (tile_mod, tile_t0, tile_len))

    cond = jnp.zeros((MOD_ROWS, d), F32).at[0].set(c_ctx).at[1:1 + bs].set(c)
    mods_all = modulation_all(cond, ada_w, ada_b).reshape(depth, MOD_ROWS, 6, 1, d)

    cos_t, sin_t = _rope_tables(ts)
    dft_p = _dft_tables(tp)
    dft_s = _dft_tables(ts)
    gmat = _block_diag(jnp.ones((ATTN_W // HEAD_DIM, HEAD_DIM, HEAD_DIM), BF16))
    tri = (jnp.arange(tm)[:, None] < jnp.arange(tm)[None, :]).astype(BF16)
    tile_w = ATTN_W // HEAD_DIM

    low = (jnp.arange(N_EXPERTS)[:, None] > jnp.arange(N_EXPERTS)[None, :]).astype(BF16)
    n_tiles = n // tm
    bm = EXPERT_BLOCK
    max_rows = n * TOP_K + n_tiles * N_EXPERTS * (SEG_ALIGN - 1)
    n_blocks = -(-max_rows // bm) + N_EXPERTS
    n_rows = n_blocks * bm

    x = (x_prompt.reshape(n_p, d), x_sample.reshape(n_s, d))
    new_k, new_v = [], []
    for l in range(depth):
        lam_init = 0.8 - 0.6 * math.exp(-0.3 * l)
        mods = mods_all[l]
        lf = lambda_qk[l]
        lam = (jnp.exp(jnp.sum(lf[0] * lf[1])) - jnp.exp(jnp.sum(lf[2] * lf[3])) + lam_init).reshape(1)

        ab = _fourier_weights(fourier_w[l]).astype(BF16)
        proj, za, zb = pre_mixer(x, tile_mod, norm1_g[l], mods, w_in[l].astype(BF16), ab)

        qg = jnp.tile(q_norm_g[l], tile_w).reshape(1, ATTN_W)
        kg = jnp.tile(k_norm_g[l], tile_w).reshape(1, ATTN_W)

        qt_p, kk_p, vt_p, kn_p = qkv_prep(proj, 0, bp, tp, gmat, qg, kg, None, None,
                                          rope=False, emit_kn=True)
        attn_p = diff_attention(lam, qt_p, kk_p, vt_p, subln_g[l], lam_init)
        new_k.append(kn_p.reshape(bp, tp, N_HEADS, 2, HEAD_DIM))
        new_v.append(proj[:n_p, 2 * ATTN_W:3 * ATTN_W].reshape(bp, tp, N_HEADS, V_DIM))

        qt_s, kk_s, vt_s = qkv_prep(proj, n_p, bs, ts, gmat, qg, kg, cos_t, sin_t,
                                    rope=True, emit_kn=False)
        ck = cache_k[:, l].reshape(bs, past, ATTN_W).astype(BF16)
        cv = cache_v[:, l].reshape(bs, past // KV_CHUNK, KV_CHUNK, N_HEADS, V_DIM).astype(BF16)
        cv = jnp.concatenate([cv.transpose(0, 1, 3, 4, 2),
                              jnp.ones((bs, past // KV_CHUNK, N_HEADS, V_ONES_ROWS, KV_CHUNK), BF16)], axis=3)
        cv = cv.reshape(bs, past // KV_CHUNK, N_HEADS * V_AUG, KV_CHUNK)
        attn_s = diff_attention(lam, qt_s, kk_s, vt_s, subln_g[l], lam_init, ctx=(ck, cv))

        pool = multiscale_pool(proj, tile_t0, tile_len, _block_diag(pool_w[l]).astype(BF16),
                               pool_scale[l])

        four_p = fourier_mix(dft_p[0], dft_p[1],
                             za[:n_p].reshape(bp, tp, FOUR_W), zb[:n_p].reshape(bp, tp, FOUR_W))
        four_s = fourier_mix(dft_s[0], dft_s[1],
                             za[n_p:].reshape(bs, ts, FOUR_W), zb[n_p:].reshape(bs, ts, FOUR_W))
        four_p, four_s = four_p.reshape(n_p, FOUR_W), four_s.reshape(n_s, FOUR_W)

        rw_pad = jnp.zeros((d, LANES), F32).at[:, :N_EXPERTS].set(router_w[l])
        rb_pad = jnp.full((1, LANES), -1e30, F32).at[0, :N_EXPERTS].set(router_b[l])
        x1, h2, pos, gp, counts = post_mixer(
            x, tile_mod, (attn_p, attn_s), pool, (four_p, four_s), w_out[l].astype(BF16), mods,
            norm2_g[l], rw_pad, rb_pad, tri, low)

        cnt = counts.reshape(n_tiles, 8, LANES)[:, 0, :N_EXPERTS].astype(jnp.int32)
        seg = (cnt + SEG_ALIGN - 1) // SEG_ALIGN * SEG_ALIGN
        off = jnp.cumsum(seg, axis=1) - seg
        rows_e = jnp.sum(seg, axis=0)
        padded = (rows_e + bm - 1) // bm * bm
        pad_end = jnp.cumsum(padded)
        pad_start = pad_end - padded
        gbase = pad_start[None, :] + jnp.cumsum(seg, axis=0) - seg
        nchunk = seg // COPY_ROWS[1]
        copy_ratio = COPY_ROWS[0] // COPY_ROWS[1]
        n_used = (pad_end[-1] // bm).astype(jnp.int32).reshape(1)
        tables = {
            "off": off.reshape(-1), "gbase": gbase.reshape(-1), "nchunk": nchunk.reshape(-1),
            "tbig": jnp.sum(nchunk // copy_ratio, axis=1), "tsmall": jnp.sum(nchunk % copy_ratio, axis=1),
            "npad": (padded - rows_e) // SEG_ALIGN,
            "padbase": pad_start + rows_e, "n_used": n_used,
        }
        block_row0 = jnp.arange(n_blocks, dtype=jnp.int32) * bm
        block_expert = jnp.minimum(
            jnp.sum((pad_end[None, :] <= block_row0[:, None]).astype(jnp.int32), axis=1), N_EXPERTS - 1)

        x_rows = moe_dispatch(h2, pos, tables, n_rows)
        next_first = (pad_end // bm)[block_expert]
        y_rows = expert_ffn(block_expert, next_first, n_used, x_rows, l, w_gate_up, b_gate_up, w_down,
                            b_down)
        x = moe_combine(x1, tile_mod, y_rows, gp, mods, tables, n_p)

    y_prompt = x[0].reshape(bp, tp, d)
    y_sample = x[1].reshape(bs, ts, d)
    return y_prompt, y_sample, jnp.stack(new_k, axis=1), jnp.stack(new_v, axis=1)
```

```python
import functools
import math

import numpy as np
import jax
import jax.numpy as jnp
from jax import lax
from jax.experimental import pallas as pl
from jax.experimental.pallas import tpu as pltpu

F32 = jnp.float32
BF16 = jnp.bfloat16

GRID_W = 64
HEAD_DIM = 64
V_DIM = 2 * HEAD_DIM
N_HEADS = 4
ATTN_W = N_HEADS * V_DIM
POOL_WINDOWS = (2, 4, 8, 16)
POOL_GROUP = 64
POOL_W = 256
FOUR_GROUPS = 4
FOUR_GROUP = 64
FOUR_W = 256
N_EXPERTS = 32
TOP_K = 4
SWIGLU_LIMIT = 7.0
SWIGLU_ALPHA = 1.702
ROPE_THETA = 10000.0
RMS_EPS = 1e-6

LANES = 128
TOKEN_TILE = 256
POOL_HALO = 8
KV_CHUNK = 256
ATTN_STEP_CHUNKS = 2
ATTN_Q_TILE = 512
ATTN_LOOKAHEAD = 2
V_ONES_ROWS = 16
V_AUG = V_DIM + V_ONES_ROWS
EXPERT_BLOCK = 256
SEG_ALIGN = 8
TILE_SLOTS = TOKEN_TILE * TOP_K + N_EXPERTS * SEG_ALIGN
COPY_ROWS = (4 * SEG_ALIGN, SEG_ALIGN)
DFT_ROW_TILE = 1024
MOD_ROWS = 16
VMEM_LIMIT = 56 * 1024 * 1024


def _dot(a, b):
    return jnp.dot(a, b, preferred_element_type=F32)


def _split_bf16(x):
    hi = x.astype(BF16)
    lo = (x - hi.astype(F32)).astype(BF16)
    return hi, lo


def _cparams(sem):
    return pltpu.CompilerParams(dimension_semantics=sem, vmem_limit_bytes=VMEM_LIMIT)


def _mod_kernel(cond_ref, w_ref, b_ref, o_ref):
    c = cond_ref[...]
    a = c * jax.nn.sigmoid(c)
    ah, al = _split_bf16(a)
    wh, wl = _split_bf16(w_ref[0])
    o_ref[0] = _dot(ah, wh) + _dot(ah, wl) + _dot(al, wh) + b_ref[0]


def modulation_all(cond, ada_w, ada_b):
    depth, d, d6 = ada_w.shape
    tn = d6 // 4
    return pl.pallas_call(
        _mod_kernel,
        out_shape=jax.ShapeDtypeStruct((depth, MOD_ROWS, d6), F32),
        grid=(depth, d6 // tn),
        in_specs=[
            pl.BlockSpec((MOD_ROWS, d), lambda l, j: (0, 0)),
            pl.BlockSpec((1, d, tn), lambda l, j: (l, 0, j)),
            pl.BlockSpec((1, 1, tn), lambda l, j: (l, 0, j)),
        ],
        out_specs=pl.BlockSpec((1, MOD_ROWS, tn), lambda l, j: (l, 0, j)),
        compiler_params=_cparams(("parallel", "parallel")),
        name="modulation",
    )(cond, ada_w, ada_b.reshape(depth, 1, d6))


def _rms_mod(x, g, scale, shift):
    ms = jnp.mean(x * x, axis=-1, keepdims=True)
    return x * lax.rsqrt(ms + RMS_EPS) * g * (1.0 + scale) + shift


def _pair_specs(block, first_tiles):
    return [pl.BlockSpec(block, lambda i, *_: (jnp.minimum(i, first_tiles - 1), 0)),
            pl.BlockSpec(block, lambda i, *_: (jnp.maximum(i - first_tiles, 0), 0))]


def _pick_group(first_ref, second_ref, first_tiles):
    return jnp.where(pl.program_id(0) < first_tiles, first_ref[...], second_ref[...])


def _pre_mixer_kernel(tmod_ref, xp_ref, xs_ref, g_ref, mod_ref, w_ref, ab_ref, proj_ref, za_ref, zb_ref,
                      *, first_tiles):
    del tmod_ref
    x = _pick_group(xp_ref, xs_ref, first_tiles)
    h = _rms_mod(x, g_ref[...], mod_ref[0, 1], mod_ref[0, 0])
    proj = _dot(h.astype(BF16), w_ref[...])
    nproj = proj_ref.shape[1]
    proj_ref[...] = proj[:, :nproj]
    z = _dot(proj[:, nproj:].astype(BF16), ab_ref[...])
    za_ref[...] = z[:, :FOUR_W].astype(BF16)
    zb_ref[...] = z[:, FOUR_W:].astype(BF16)


def pre_mixer(x, tile_mod, n1, mods, w_in_b, ab_b):
    d = x[0].shape[1]
    n = x[0].shape[0] + x[1].shape[0]
    in_w = w_in_b.shape[1]
    nproj = in_w - FOUR_W
    tm = TOKEN_TILE
    first_tiles = x[0].shape[0] // tm
    return pl.pallas_call(
        functools.partial(_pre_mixer_kernel, first_tiles=first_tiles),
        out_shape=(jax.ShapeDtypeStruct((n, nproj), F32),
                   jax.ShapeDtypeStruct((n, FOUR_W), BF16),
                   jax.ShapeDtypeStruct((n, FOUR_W), BF16)),
        grid_spec=pltpu.PrefetchScalarGridSpec(
            num_scalar_prefetch=1,
            grid=(n // tm,),
            in_specs=_pair_specs((tm, d), first_tiles) + [
                pl.BlockSpec((1, d), lambda i, tmod: (0, 0)),
                pl.BlockSpec((1, 6, 1, d), lambda i, tmod: (tmod[i], 0, 0, 0)),
                pl.BlockSpec((d, in_w), lambda i, tmod: (0, 0)),
                pl.BlockSpec((FOUR_W, 2 * FOUR_W), lambda i, tmod: (0, 0)),
            ],
            out_specs=(pl.BlockSpec((tm, nproj), lambda i, tmod: (i, 0)),
                       pl.BlockSpec((tm, FOUR_W), lambda i, tmod: (i, 0)),
                       pl.BlockSpec((tm, FOUR_W), lambda i, tmod: (i, 0))),
        ),
        compiler_params=_cparams(("parallel",)),
        name="pre_mixer",
    )(tile_mod, x[0], x[1], n1.reshape(1, d), mods, w_in_b, ab_b)


def _group_rms(x, gmat, g):
    ss = _dot((x * x).astype(BF16), gmat)
    return x * lax.rsqrt(ss * (1.0 / HEAD_DIM) + RMS_EPS) * g


def _rope(y, cos, sin_signed):
    w = y.shape[1]
    q = HEAD_DIM // 4
    lane = lax.broadcasted_iota(jnp.int32, y.shape, 1)
    partner = jnp.where(lane % (2 * q) < q, pltpu.roll(y, w - q, axis=1), pltpu.roll(y, q, axis=1))
    return y * cos + partner * sin_signed


def _qkv_kernel(q_ref, k_ref, v_ref, gmat_ref, qg_ref, kg_ref, *rest, rope, emit_kn):
    if rope:
        cos_ref, sin_ref, *outs = rest
    else:
        outs = rest
    qt_ref, kk_ref, vt_ref = outs[:3]
    gmat = gmat_ref[...]
    qn = _group_rms(q_ref[...], gmat, qg_ref[...])
    kn = _group_rms(k_ref[...], gmat, kg_ref[...])
    if emit_kn:
        outs[3][0] = kn
    if rope:
        cos = cos_ref[...]
        sin = sin_ref[...]
        qn = _rope(qn, cos, sin)
        kn = _rope(kn, cos, sin)
    qt_ref[0] = (qn * (HEAD_DIM ** -0.5 * math.log2(math.e))).T.astype(BF16)
    kk_ref[0] = kn.astype(BF16)
    vt = v_ref[...].T.astype(BF16)
    ones = jnp.ones((V_ONES_ROWS, vt.shape[1]), BF16)
    pieces = []
    for h in range(N_HEADS):
        pieces += [vt[h * V_DIM:(h + 1) * V_DIM], ones]
    vt_ref[0, 0] = jnp.concatenate(pieces, axis=0)


def qkv_prep(proj, row_off, b, t, gmat, qg, kg, cos, sin, *, rope, emit_kn):
    tm = TOKEN_TILE
    nt = t // tm
    off = row_off // tm
    w = ATTN_W
    row = lambda bi, j: (off + bi * nt + j)
    in_specs = [
        pl.BlockSpec((tm, w), lambda bi, j: (row(bi, j), 0)),
        pl.BlockSpec((tm, w), lambda bi, j: (row(bi, j), 1)),
        pl.BlockSpec((tm, w), lambda bi, j: (row(bi, j), 2)),
        pl.BlockSpec((w, w), lambda bi, j: (0, 0)),
        pl.BlockSpec((1, w), lambda bi, j: (0, 0)),
        pl.BlockSpec((1, w), lambda bi, j: (0, 0)),
    ]
    args = [proj, proj, proj, gmat, qg, kg]
    if rope:
        in_specs += [pl.BlockSpec((tm, w), lambda bi, j: (j, 0)),
                     pl.BlockSpec((tm, w), lambda bi, j: (j, 0))]
        args += [cos, sin]
    out_shape = [jax.ShapeDtypeStruct((b, w, t), BF16),
                 jax.ShapeDtypeStruct((b, t, w), BF16),
                 jax.ShapeDtypeStruct((b, nt, N_HEADS * V_AUG, tm), BF16)]
    out_specs = [pl.BlockSpec((1, w, tm), lambda bi, j: (bi, 0, j)),
                 pl.BlockSpec((1, tm, w), lambda bi, j: (bi, j, 0)),
                 pl.BlockSpec((1, 1, N_HEADS * V_AUG, tm), lambda bi, j: (bi, j, 0, 0))]
    if emit_kn:
        out_shape.append(jax.ShapeDtypeStruct((b, t, w), F32))
        out_specs.append(pl.BlockSpec((1, tm, w), lambda bi, j: (bi, j, 0)))
    return pl.pallas_call(
        functools.partial(_qkv_kernel, rope=rope, emit_kn=emit_kn),
        out_shape=tuple(out_shape),
        grid=(b, nt),
        in_specs=in_specs,
        out_specs=tuple(out_specs),
        compiler_params=_cparams(("parallel", "parallel")),
        name="qkv_prep_rope" if rope else "qkv_prep",
    )(*args)


def _attn_kernel(lam_ref, qt_ref, k_ref, vt_ref, *rest, n_own, n_ctx, sub, lam_init):
    if n_ctx:
        kc_ref, vc_ref, g_ref, o_ref = rest
    else:
        g_ref, o_ref = rest
    q = qt_ref[0]
    row = lax.broadcasted_iota(jnp.int32, q.shape, 0)
    zero = jnp.zeros_like(q)
    qmaps = (jnp.where(row < HEAD_DIM, q, zero), jnp.where(row >= HEAD_DIM, q, zero))
    tk = sub * KV_CHUNK
    steps = [(k_ref, vt_ref, c) for c in range(n_own)] + [(kc_ref, vc_ref, c) for c in range(n_ctx)]
    nk = len(steps)

    def scores(step):
        kr, _, c = steps[step]
        k = kr[0, c * tk:(c + 1) * tk, :]
        return [_dot(k, qm) for qm in qmaps]

    m_run, acc = [None, None], [None, None]
    pending = [scores(j) for j in range(min(ATTN_LOOKAHEAD, nk))]
    for c in range(nk):
        s_cur = pending.pop(0)
        if c + ATTN_LOOKAHEAD < nk:
            pending.append(scores(c + ATTN_LOOKAHEAD))
        _, vr, cv = steps[c]
        for m in range(2):
            s = s_cur[m]
            m_blk = jnp.max(s, axis=0, keepdims=True)
            m_new = m_blk if c == 0 else jnp.maximum(m_run[m], m_blk)
            pb = jnp.exp2(s - m_new).astype(BF16)
            pv = _dot(vr[0, cv * sub], pb[:KV_CHUNK])
            for j in range(1, sub):
                pv = pv + _dot(vr[0, cv * sub + j], pb[j * KV_CHUNK:(j + 1) * KV_CHUNK])
            if c == 0:
                acc[m] = pv
            else:
                acc[m] = jnp.exp2(m_run[m] - m_new) * acc[m] + pv
            m_run[m] = m_new

    outs = [a[:V_DIM] / a[V_DIM:V_DIM + 1] for a in acc]
    o = outs[0] - lam_ref[0] * outs[1]
    ms = jnp.mean(o * o, axis=0, keepdims=True)
    o = o * lax.rsqrt(ms + RMS_EPS) * g_ref[...] * (1.0 - lam_init)
    o_ref[...] = o.T.astype(BF16)


def diff_attention(lam, qt, kk, vt, subln_g, lam_init, ctx=None):
    b, w, t = qt.shape
    tq = min(ATTN_Q_TILE, t)
    nq = t // tq
    hw = V_DIM
    kv = [(kk, vt)] + ([ctx] if ctx is not None else [])
    chunks = [k.shape[1] // KV_CHUNK for k, _ in kv]
    sub = ATTN_STEP_CHUNKS if all(c % ATTN_STEP_CHUNKS == 0 for c in chunks) else 1
    in_specs = [pl.BlockSpec((1, hw, tq), lambda bi, h, qi, lam: (bi, h, qi))]
    args = [qt]
    for (k, v), nc in zip(kv, chunks):
        in_specs += [pl.BlockSpec((1, k.shape[1], hw), lambda bi, h, qi, lam: (bi, 0, h)),
                     pl.BlockSpec((1, nc, V_AUG, KV_CHUNK), lambda bi, h, qi, lam: (bi, 0, h, 0))]
        args += [k, v]
    in_specs.append(pl.BlockSpec((hw, 1), lambda bi, h, qi, lam: (0, 0)))
    return pl.pallas_call(
        functools.partial(_attn_kernel, n_own=chunks[0] // sub,
                          n_ctx=chunks[1] // sub if ctx is not None else 0, sub=sub, lam_init=lam_init),
        out_shape=jax.ShapeDtypeStruct((b * t, w), BF16),
        grid_spec=pltpu.PrefetchScalarGridSpec(
            num_scalar_prefetch=1,
            grid=(b, N_HEADS, nq),
            in_specs=in_specs,
            out_specs=pl.BlockSpec((tq, hw), lambda bi, h, qi, lam: (bi * nq + qi, h)),
        ),
        compiler_params=_cparams(("parallel", "parallel", "parallel")),
        name="diff_attention",
    )(lam, *args, subln_g.reshape(hw, 1))


def _pool_kernel(t0_ref, tlen_ref, prev_ref, cur_ref, next_ref, pw_ref, ps_ref, o_ref):
    i = pl.program_id(0)
    t0 = t0_ref[i]
    tlen = tlen_ref[i]
    tm = cur_ref.shape[0]
    u = cur_ref[...]
    prev = jnp.where(t0 > 0, prev_ref[...], 0.0)
    nxt = jnp.where(t0 + tm < tlen, next_ref[...], 0.0)
    e = jnp.concatenate([prev, u, nxt], axis=0)
    n = e.shape[0]
    w2 = e + pltpu.roll(e, 1, axis=0)
    w4 = pltpu.roll(w2, 1, axis=0) + pltpu.roll(w2, n - 1, axis=0)
    w8 = pltpu.roll(w4, 2, axis=0) + pltpu.roll(w4, n - 2, axis=0)
    w16 = pltpu.roll(w8, 4, axis=0) + pltpu.roll(w8, n - 4, axis=0)
    sums = [w[POOL_HALO:POOL_HALO + tm] for w in (w2, w4, w8, w16)]
    lane = lax.broadcasted_iota(jnp.int32, u.shape, 1)
    grp = lane // POOL_GROUP
    tabs = t0 + lax.broadcasted_iota(jnp.int32, u.shape, 0)
    total = sums[3]
    half = jnp.full(u.shape, POOL_WINDOWS[3] // 2, jnp.int32)
    for gi in (2, 1, 0):
        total = jnp.where(grp == gi, sums[gi], total)
        half = jnp.where(grp == gi, POOL_WINDOWS[gi] // 2, half)
    cnt = jnp.minimum(tabs + half, tlen) - jnp.maximum(tabs - half, 0)
    d = total / cnt.astype(F32) - u
    o_ref[...] = (_dot(d.astype(BF16), pw_ref[...]) * ps_ref[...]).astype(BF16)


def multiscale_pool(proj, tile_t0, tile_len, pw_b, pool_scale):
    n = proj.shape[0]
    tm = TOKEN_TILE
    hb = tm // POOL_HALO
    col = (3 * ATTN_W) // POOL_W
    last = n // POOL_HALO - 1
    return pl.pallas_call(
        _pool_kernel,
        out_shape=jax.ShapeDtypeStruct((n, POOL_W), BF16),
        grid_spec=pltpu.PrefetchScalarGridSpec(
            num_scalar_prefetch=2,
            grid=(n // tm,),
            in_specs=[
                pl.BlockSpec((POOL_HALO, POOL_W), lambda i, a, b: (jnp.maximum(i * hb - 1, 0), col)),
                pl.BlockSpec((tm, POOL_W), lambda i, a, b: (i, col)),
                pl.BlockSpec((POOL_HALO, POOL_W), lambda i, a, b: (jnp.minimum((i + 1) * hb, last), col)),
                pl.BlockSpec((POOL_W, POOL_W), lambda i, a, b: (0, 0)),
                pl.BlockSpec((1, POOL_W), lambda i, a, b: (0, 0)),
            ],
            out_specs=pl.BlockSpec((tm, POOL_W), lambda i, a, b: (i, 0)),
        ),
        compiler_params=_cparams(("parallel",)),
        name="multiscale_pool",
    )(tile_t0, tile_len, proj, proj, proj, pw_b, pool_scale.reshape(1, POOL_W))


def _fourier_kernel(c_ref, s_ref, za_ref, zb_ref, o_ref):
    o_ref[0] = (_dot(c_ref[...], za_ref[0]) - _dot(s_ref[...], zb_ref[0])).astype(BF16)


def fourier_mix(cmat, smat, za, zb):
    b, t, w = za.shape
    tr = min(DFT_ROW_TILE, t)
    return pl.pallas_call(
        _fourier_kernel,
        out_shape=jax.ShapeDtypeStruct((b, t, w), BF16),
        grid=(t // tr, b),
        in_specs=[
            pl.BlockSpec((tr, t), lambda r, bi: (r, 0)),
            pl.BlockSpec((tr, t), lambda r, bi: (r, 0)),
            pl.BlockSpec((1, t, w), lambda r, bi: (bi, 0, 0)),
            pl.BlockSpec((1, t, w), lambda r, bi: (bi, 0, 0)),
        ],
        out_specs=pl.BlockSpec((1, tr, w), lambda r, bi: (bi, r, 0)),
        compiler_params=_cparams(("parallel", "parallel")),
        name="fourier_mix",
    )(cmat, smat, za, zb)


def _post_mixer_kernel(tmod_ref, xp_ref, xs_ref, ap_ref, as_ref, p_ref, fp_ref, fs_ref, wo_ref, mod_ref,
                       g2_ref, rw_ref, rb_ref, tri_ref, low_ref, x1_ref, h2_ref, pos_ref, gp_ref, cnt_ref,
                       *, first_tiles):
    del tmod_ref
    attn = _pick_group(ap_ref, as_ref, first_tiles)
    four = _pick_group(fp_ref, fs_ref, first_tiles)
    mixed = jnp.concatenate([attn, p_ref[...], four], axis=1)
    x1 = _pick_group(xp_ref, xs_ref, first_tiles) + mod_ref[0, 2] * _dot(mixed, wo_ref[...])
    x1_ref[...] = x1
    h2 = _rms_mod(x1, g2_ref[...], mod_ref[0, 4], mod_ref[0, 3])
    hh, hl = _split_bf16(h2)
    h2_ref[...] = hh

    rw = rw_ref[...]
    rwh, rwl = _split_bf16(rw)
    logits = _dot(hh, rwh) + _dot(hh, rwl) + _dot(hl, rwh) + rb_ref[...]
    lt = logits.T[:N_EXPERTS]
    eidx = lax.broadcasted_iota(jnp.int32, lt.shape, 0).astype(F32)
    work = lt
    vals, hots = [], []
    for _ in range(TOP_K):
        mx = jnp.max(work, axis=0, keepdims=True)
        ix = jnp.min(jnp.where(work == mx, eidx, float(N_EXPERTS)), axis=0, keepdims=True)
        hot = eidx == ix
        work = jnp.where(hot, -jnp.inf, work)
        vals.append(mx)
        hots.append(hot)
    exps = [jnp.exp(v - vals[0]) for v in vals]
    denom = exps[0] + exps[1] + exps[2] + exps[3]
    sel = (hots[0] | hots[1] | hots[2] | hots[3]).astype(F32)
    selb = sel.astype(BF16)
    prefix = _dot(selb, tri_ref[...])
    cnt_col = jnp.sum(sel, axis=1, keepdims=True)
    seg_col = jnp.ceil(cnt_col * (1.0 / SEG_ALIGN)) * SEG_ALIGN
    seg_start = _dot(low_ref[...], jnp.broadcast_to(seg_col, (N_EXPERTS, LANES)).astype(BF16))[:, 0:1]
    slot = prefix + seg_start
    tm = lt.shape[1]
    pos = [jnp.sum(jnp.where(h, slot, 0.0), axis=0, keepdims=True) for h in hots]
    pos_ref[...] = jnp.concatenate(pos + [jnp.zeros((8 - TOP_K, tm), F32)], axis=0).astype(jnp.int32)
    gp_t = jnp.concatenate([e / denom for e in exps] + pos
                           + [jnp.zeros((LANES - 2 * TOP_K, tm), F32)], axis=0)
    gp_ref[...] = gp_t.T
    sel_pad = jnp.concatenate([selb, jnp.zeros((LANES - N_EXPERTS, tm), BF16)], axis=0)
    cnt_ref[...] = lax.dot_general(jnp.ones((8, tm), BF16), sel_pad, (((1,), (1,)), ((), ())),
                                   preferred_element_type=F32)


def post_mixer(x, tile_mod, attn, pool, four, wo_b, mods, n2, rw_pad, rb_pad, tri, low):
    d = x[0].shape[1]
    n = x[0].shape[0] + x[1].shape[0]
    tm = TOKEN_TILE
    first_tiles = attn[0].shape[0] // tm
    tok = lambda i, tmod: (i, 0)
    const = lambda i, tmod: (0, 0)
    lane_tok = lambda i, tmod: (0, i)
    return pl.pallas_call(
        functools.partial(_post_mixer_kernel, first_tiles=first_tiles),
        out_shape=(jax.ShapeDtypeStruct((n, d), F32),
                   jax.ShapeDtypeStruct((n, d), BF16),
                   jax.ShapeDtypeStruct((8, n), jnp.int32),
                   jax.ShapeDtypeStruct((n, LANES), F32),
                   jax.ShapeDtypeStruct((n // tm * 8, LANES), F32)),
        grid_spec=pltpu.PrefetchScalarGridSpec(
            num_scalar_prefetch=1,
            grid=(n // tm,),
            in_specs=_pair_specs((tm, d), first_tiles) + _pair_specs((tm, ATTN_W), first_tiles) + [
                pl.BlockSpec((tm, POOL_W), tok),
            ] + _pair_specs((tm, FOUR_W), first_tiles) + [
                pl.BlockSpec((d, d), const),
                pl.BlockSpec((1, 6, 1, d), lambda i, tmod: (tmod[i], 0, 0, 0)),
                pl.BlockSpec((1, d), const),
                pl.BlockSpec((d, LANES), const),
                pl.BlockSpec((1, LANES), const),
                pl.BlockSpec((tm, tm), const),
                pl.BlockSpec((N_EXPERTS, N_EXPERTS), const),
            ],
            out_specs=(pl.BlockSpec((tm, d), tok),
                       pl.BlockSpec((tm, d), tok),
                       pl.BlockSpec((8, tm), lane_tok),
                       pl.BlockSpec((tm, LANES), tok),
                       pl.BlockSpec((8, LANES), tok)),
        ),
        compiler_params=_cparams(("parallel",)),
        name="post_mixer_router",
    )(tile_mod, x[0], x[1], attn[0], attn[1], pool, four[0], four[1], wo_b, mods, n2.reshape(1, d),
      rw_pad, rb_pad, tri, low)


def _slot_copy_loops(i, off_ref, gbase_ref, nchunk_ref, start_copy):
    ratio = COPY_ROWS[0] // COPY_ROWS[1]

    def per_expert(e, carry):
        idx = i * N_EXPERTS + e
        local0 = off_ref[idx]
        global0 = gbase_ref[idx]
        n_small_units = nchunk_ref[idx]
        n_big = n_small_units // ratio
        rest0 = n_big * COPY_ROWS[0]

        def big(j, c):
            start_copy(pl.multiple_of(local0 + j * COPY_ROWS[0], SEG_ALIGN),
                       pl.multiple_of(global0 + j * COPY_ROWS[0], SEG_ALIGN), 0)
            return c

        def small(j, c):
            start_copy(pl.multiple_of(local0 + rest0 + j * COPY_ROWS[1], SEG_ALIGN),
                       pl.multiple_of(global0 + rest0 + j * COPY_ROWS[1], SEG_ALIGN), 1)
            return c

        lax.fori_loop(0, n_big, big, 0)
        lax.fori_loop(0, n_small_units - n_big * ratio, small, 0)
        return carry

    lax.fori_loop(0, N_EXPERTS, per_expert, 0)


def _wait_copies(copy, count_big, count_small):
    def wait_big(j, c):
        copy(0, 0, 0).wait()
        return c

    def wait_small(j, c):
        copy(0, 0, 1).wait()
        return c

    lax.fori_loop(0, count_big, wait_big, 0)
    lax.fori_loop(0, count_small, wait_small, 0)


def _dispatch_kernel(off_ref, gbase_ref, nchunk_ref, tbig_ref, tsmall_ref, npad_ref, padbase_ref,
                     nused_ref, h2_ref, pos_ref, xrows_ref, xs, zbuf, sem):
    i = pl.program_id(0)
    last = pl.num_programs(0) - 1
    slot = i % 2
    pos = pos_ref[...]
    rows = lax.broadcasted_iota(jnp.int32, (xs.shape[1], pos.shape[1]), 0)
    hit = rows == pos[0:1]
    for k in range(1, TOP_K):
        hit = hit | (rows == pos[k:k + 1])
    xs[slot] = _dot(hit.astype(BF16), h2_ref[...])

    def seg_copy(s, local, glob, size):
        rows = COPY_ROWS[size]
        return pltpu.make_async_copy(xs.at[s, pl.ds(local, rows)],
                                     xrows_ref.at[pl.ds(glob, rows)], sem.at[s, size])

    _slot_copy_loops(i, off_ref, gbase_ref, nchunk_ref,
                     lambda local, glob, size: seg_copy(slot, local, glob, size).start())

    @pl.when(i > 0)
    def _():
        prev = jnp.maximum(i - 1, 0)
        _wait_copies(functools.partial(seg_copy, 1 - slot), tbig_ref[prev], tsmall_ref[prev])

    @pl.when(i == last)
    def _():
        _wait_copies(functools.partial(seg_copy, slot), tbig_ref[i], tsmall_ref[i])
        zbuf[...] = jnp.zeros(zbuf.shape, zbuf.dtype)

        def zero_copy(dst, rows):
            return pltpu.make_async_copy(zbuf.at[pl.ds(0, rows)], xrows_ref.at[pl.ds(dst, rows)],
                                         sem.at[slot, 0])

        def per_expert(e, total):
            def per_chunk(j, c):
                zero_copy(pl.multiple_of(padbase_ref[e] + j * SEG_ALIGN, SEG_ALIGN), SEG_ALIGN).start()
                return c
            lax.fori_loop(0, npad_ref[e], per_chunk, 0)
            return total + npad_ref[e]

        total = lax.fori_loop(0, N_EXPERTS, per_expert, 0)

        def wait_zero(j, c):
            zero_copy(0, SEG_ALIGN).wait()
            return c

        lax.fori_loop(0, total, wait_zero, 0)

        bm = zbuf.shape[0]
        n_blocks = xrows_ref.shape[0] // bm

        def zero_block(b, c):
            zero_copy(pl.multiple_of(b * bm, bm), bm).start()
            return c

        def wait_block(b, c):
            zero_copy(0, bm).wait()
            return c

        lax.fori_loop(nused_ref[0], n_blocks, zero_block, 0)
        lax.fori_loop(nused_ref[0], n_blocks, wait_block, 0)


def moe_dispatch(h2, pos, tables, n_rows):
    n, d = h2.shape
    tm = TOKEN_TILE
    return pl.pallas_call(
        _dispatch_kernel,
        out_shape=jax.ShapeDtypeStruct((n_rows, d), F32),
        grid_spec=pltpu.PrefetchScalarGridSpec(
            num_scalar_prefetch=8,
            grid=(n // tm,),
            in_specs=[
                pl.BlockSpec((tm, d), lambda i, *_: (i, 0)),
                pl.BlockSpec((8, tm), lambda i, *_: (0, i)),
            ],
            out_specs=pl.BlockSpec(memory_space=pl.ANY),
            scratch_shapes=[pltpu.VMEM((2, TILE_SLOTS, d), F32),
                            pltpu.VMEM((EXPERT_BLOCK, d), F32),
                            pltpu.SemaphoreType.DMA((2, 2))],
        ),
        compiler_params=_cparams(("arbitrary",)),
        name="moe_dispatch",
    )(tables["off"], tables["gbase"], tables["nchunk"], tables["tbig"], tables["tsmall"],
      tables["npad"], tables["padbase"], tables["n_used"], h2, pos)


def _expert_kernel(be_ref, nxt_ref, nused_ref, x_ref, wgu_ref, bgu_ref, wd_ref, bd_ref, y_ref,
                   stage_gu, stage_d, wgu_s, wd_s, sem, *, layer):
    b = pl.program_id(0)
    n_used = nused_ref[0]

    def weight_copies(e):
        return (pltpu.make_async_copy(wgu_ref.at[layer, e], stage_gu, sem.at[0]),
                pltpu.make_async_copy(wd_ref.at[layer, e], stage_d, sem.at[1]))

    first = ((b == 0) | (be_ref[b] != be_ref[jnp.maximum(b - 1, 0)])) & (b < n_used)

    @pl.when(first)
    def _():
        @pl.when(b == 0)
        def _():
            for c in weight_copies(be_ref[0]):
                c.start()

        for c in weight_copies(be_ref[b]):
            c.wait()
        wgu_s[...] = stage_gu[...].astype(BF16)
        wd_s[...] = stage_d[...].astype(BF16)
        nb = nxt_ref[b]

        @pl.when(nb < n_used)
        def _():
            for c in weight_copies(be_ref[jnp.minimum(nb, pl.num_programs(0) - 1)]):
                c.start()

    @pl.when(b < nused_ref[0])
    def _():
        gu = _dot(x_ref[...].astype(BF16), wgu_s[...]) + bgu_ref[0]
        f = gu.shape[1] // 2
        glu = jnp.minimum(gu[:, :f], SWIGLU_LIMIT)
        lin = jnp.clip(gu[:, f:], -SWIGLU_LIMIT, SWIGLU_LIMIT)
        act = glu * jax.nn.sigmoid(SWIGLU_ALPHA * glu) * (lin + 1.0)
        y_ref[...] = _dot(act.astype(BF16), wd_s[...]) + bd_ref[0]

    @pl.when(b >= nused_ref[0])
    def _():
        y_ref[...] = jnp.zeros(y_ref.shape, y_ref.dtype)


def expert_ffn(block_expert, next_first, n_used, x_rows, layer, wgu, bgu, wd, bd):
    n_rows, d = x_rows.shape
    depth, e, _, f2 = wgu.shape
    bm = EXPERT_BLOCK
    return pl.pallas_call(
        functools.partial(_expert_kernel, layer=layer),
        out_shape=jax.ShapeDtypeStruct((n_rows, d), F32),
        grid_spec=pltpu.PrefetchScalarGridSpec(
            num_scalar_prefetch=3,
            grid=(n_rows // bm,),
            in_specs=[
                pl.BlockSpec((bm, d), lambda b, be, nx, nu: (jnp.minimum(b, nu[0] - 1), 0)),
                pl.BlockSpec(memory_space=pl.ANY),
                pl.BlockSpec((1, 1, f2), lambda b, be, nx, nu: (layer * e + be[b], 0, 0)),
                pl.BlockSpec(memory_space=pl.ANY),
                pl.BlockSpec((1, 1, d), lambda b, be, nx, nu: (layer * e + be[b], 0, 0)),
            ],
            out_specs=pl.BlockSpec((bm, d), lambda b, be, nx, nu: (b, 0)),
            scratch_shapes=[pltpu.VMEM((d, f2), F32), pltpu.VMEM((f2 // 2, d), F32),
                            pltpu.VMEM((d, f2), BF16), pltpu.VMEM((f2 // 2, d), BF16),
                            pltpu.SemaphoreType.DMA((2,))],
        ),
        compiler_params=_cparams(("arbitrary",)),
        name="expert_ffn",
    )(block_expert, next_first, n_used, x_rows, wgu, bgu.reshape(depth * e, 1, f2), wd,
      bd.reshape(depth * e, 1, d))


def _combine_kernel(tmod_ref, off_ref, gbase_ref, nchunk_ref, tbig_ref, tsmall_ref,
                    x_ref, gp_ref, mod_ref, yrows_ref, op_ref, os_ref, ys, sem, *, first_tiles):
    del tmod_ref
    i = pl.program_id(0)
    last = pl.num_programs(0) - 1
    slot = i % 2

    def seg_copy(s, local, glob, size):
        rows = COPY_ROWS[size]
        return pltpu.make_async_copy(yrows_ref.at[pl.ds(glob, rows)],
                                     ys.at[s, pl.ds(local, rows)], sem.at[s, size])

    def fetch(tile, s):
        _slot_copy_loops(tile, off_ref, gbase_ref, nchunk_ref,
                         lambda local, glob, size: seg_copy(s, local, glob, size).start())

    @pl.when(i == 0)
    def _():
        ys[...] = jnp.zeros(ys.shape, ys.dtype)
        fetch(0, 0)

    @pl.when(i < last)
    def _():
        fetch(i + 1, 1 - slot)

    _wait_copies(functools.partial(seg_copy, slot), tbig_ref[i], tsmall_ref[i])

    gp = gp_ref[...]
    lanes = lax.broadcasted_iota(jnp.int32, (gp.shape[0], ys.shape[1]), 1).astype(F32)
    wmat = jnp.where(lanes == gp[:, TOP_K:TOP_K + 1], gp[:, 0:1], 0.0)
    for k in range(1, TOP_K):
        wmat = jnp.where(lanes == gp[:, TOP_K + k:TOP_K + k + 1], gp[:, k:k + 1], wmat)
    x2 = x_ref[...] + mod_ref[0, 5] * _dot(wmat.astype(BF16), ys[slot].astype(BF16))

    @pl.when(i < first_tiles)
    def _():
        op_ref[...] = x2

    @pl.when(i >= first_tiles)
    def _():
        os_ref[...] = x2


def moe_combine(x1, tile_mod, y_rows, gp, mods, tables, first_rows):
    n, d = x1.shape
    tm = TOKEN_TILE
    first_tiles = first_rows // tm
    return pl.pallas_call(
        functools.partial(_combine_kernel, first_tiles=first_tiles),
        out_shape=(jax.ShapeDtypeStruct((first_rows, d), F32),
                   jax.ShapeDtypeStruct((n - first_rows, d), F32)),
        grid_spec=pltpu.PrefetchScalarGridSpec(
            num_scalar_prefetch=6,
            grid=(n // tm,),
            in_specs=[
                pl.BlockSpec((tm, d), lambda i, *_: (i, 0)),
                pl.BlockSpec((tm, LANES), lambda i, *_: (i, 0)),
                pl.BlockSpec((1, 6, 1, d), lambda i, tmod, *_: (tmod[i], 0, 0, 0)),
                pl.BlockSpec(memory_space=pl.ANY),
            ],
            out_specs=tuple(_pair_specs((tm, d), first_tiles)),
            scratch_shapes=[pltpu.VMEM((2, TILE_SLOTS, d), F32),
                            pltpu.SemaphoreType.DMA((2, 2))],
        ),
        compiler_params=_cparams(("arbitrary",)),
        name="moe_combine",
    )(tile_mod, tables["off"], tables["gbase"], tables["nchunk"], tables["tbig"], tables["tsmall"],
      x1, gp, mods, y_rows)


def _rope_tables(t):
    quarter = HEAD_DIM // 4
    lane = jnp.arange(ATTN_W)
    inv_freq = ROPE_THETA ** (-(lane % quarter).astype(F32) / quarter)
    pos_t = jnp.arange(t)
    pos = jnp.where((lane % HEAD_DIM)[None, :] < HEAD_DIM // 2,
                    (pos_t // GRID_W)[:, None], (pos_t % GRID_W)[:, None]).astype(F32)
    ang = pos * inv_freq[None, :]
    sign = jnp.where(lane % (2 * quarter) < quarter, -1.0, 1.0).astype(F32)
    return jnp.cos(ang), jnp.sin(ang) * sign[None, :]


def _dft_tables(t):
    scale = 1.0 / math.sqrt(t)
    pos = jnp.arange(t, dtype=jnp.int32)

    def angles(freq, period):
        return ((freq[:, None] * pos[None, :]) % period).astype(F32) * (2.0 * math.pi / period)

    r = math.isqrt(t)
    if r * r != t:
        ang = angles(pos, t)
        return (jnp.cos(ang) * scale).astype(BF16), (jnp.sin(ang) * scale).astype(BF16)
    sub = jnp.arange(r, dtype=jnp.int32)
    hi, lo = angles(sub, r), angles(sub, t)
    ch, sh = jnp.cos(hi)[:, None, :] * scale, jnp.sin(hi)[:, None, :] * scale
    cl, sl = jnp.cos(lo)[None, :, :], jnp.sin(lo)[None, :, :]
    return ((ch * cl - sh * sl).reshape(t, t).astype(BF16),
            (sh * cl + ch * sl).reshape(t, t).astype(BF16))


def _block_diag(blocks):
    g, c, d = blocks.shape
    eye = jnp.eye(g, dtype=blocks.dtype)
    return (eye[:, None, :, None] * blocks[:, :, None, :]).reshape(g * c, g * d)


def _fourier_weights(fourier_w_l):
    c = jnp.arange(FOUR_GROUP, dtype=jnp.int32)
    ang = ((c[:, None] * c[None, :]) % FOUR_GROUP).astype(F32) * (2.0 * math.pi / FOUR_GROUP)
    scale = 1.0 / math.sqrt(FOUR_GROUP)
    hp = lax.Precision.HIGHEST
    a = jnp.einsum("ck,gkd->gcd", jnp.cos(ang) * scale, fourier_w_l, precision=hp)
    b = jnp.einsum("ck,gkd->gcd", jnp.sin(ang) * scale, fourier_w_l, precision=hp)
    return jnp.concatenate([_block_diag(a), _block_diag(b)], axis=1)


def kernel(x_prompt, x_sample, c, c_ctx, cache_k, cache_v, ada_w, ada_b, norm1_g, norm2_g, w_in,
           q_norm_g, k_norm_g, lambda_qk, subln_g, pool_w, pool_scale, fourier_w, w_out,
           router_w, router_b, w_gate_up, b_gate_up, w_down, b_down):
    bp, tp, d = x_prompt.shape
    bs, ts, _ = x_sample.shape
    depth = ada_w.shape[0]
    past = cache_k.shape[2]
    tm = TOKEN_TILE
    n_p, n_s = bp * tp, bs * ts
    n = n_p + n_s
    assert tp % tm == 0 and ts % tm == 0 and past % KV_CHUNK == 0 and bs + 1 <= MOD_ROWS
    assert ts % min(ATTN_Q_TILE, ts) == 0 and tp % min(ATTN_Q_TILE, tp) == 0

    tile_mod = np.concatenate([np.zeros(n_p // tm, np.int32),
                               1 + np.repeat(np.arange(bs, dtype=np.int32), ts // tm)])
    tile_t0 = np.concatenate([np.tile(np.arange(0, tp, tm, dtype=np.int32), bp),
                              np.tile(np.arange(0, ts, tm, dtype=np.int32), bs)])
    tile_len = np.concatenate([np.full(n_p // tm, tp, np.int32), np.full(n_s // tm, ts, np.int32)])
    tile_mod, tile_t0, tile_len = map(jnp.asarray, (tile_mod, tile_t0, tile_len))

    cond = jnp.zeros((MOD_ROWS, d), F32).at[0].set(c_ctx).at[1:1 + bs].set(c)
    mods_all = modulation_all(cond, ada_w, ada_b).reshape(depth, MOD_ROWS, 6, 1, d)

    cos_t, sin_t = _rope_tables(ts)
    dft_p = _dft_tables(tp)
    dft_s = _dft_tables(ts)
    gmat = _block_diag(jnp.ones((ATTN_W // HEAD_DIM, HEAD_DIM, HEAD_DIM), BF16))
    tri = (jnp.arange(tm)[:, None] < jnp.arange(tm)[None, :]).astype(BF16)
    tile_w = ATTN_W // HEAD_DIM

    low = (jnp.arange(N_EXPERTS)[:, None] > jnp.arange(N_EXPERTS)[None, :]).astype(BF16)
    n_tiles = n // tm
    bm = EXPERT_BLOCK
    max_rows = n * TOP_K + n_tiles * N_EXPERTS * (SEG_ALIGN - 1)
    n_blocks = -(-max_rows // bm) + N_EXPERTS
    n_rows = n_blocks * bm

    x = (x_prompt.reshape(n_p, d), x_sample.reshape(n_s, d))
    new_k, new_v = [], []
    for l in range(depth):
        lam_init = 0.8 - 0.6 * math.exp(-0.3 * l)
        mods = mods_all[l]
        lf = lambda_qk[l]
        lam = (jnp.exp(jnp.sum(lf[0] * lf[1])) - jnp.exp(jnp.sum(lf[2] * lf[3])) + lam_init).reshape(1)

        ab = _fourier_weights(fourier_w[l]).astype(BF16)
        proj, za, zb = pre_mixer(x, tile_mod, norm1_g[l], mods, w_in[l].astype(BF16), ab)

        qg = jnp.tile(q_norm_g[l], tile_w).reshape(1, ATTN_W)
        kg = jnp.tile(k_norm_g[l], tile_w).reshape(1, ATTN_W)

        qt_p, kk_p, vt_p, kn_p = qkv_prep(proj, 0, bp, tp, gmat, qg, kg, None, None,
                                          rope=False, emit_kn=True)
        attn_p = diff_attention(lam, qt_p, kk_p, vt_p, subln_g[l], lam_init)
        new_k.append(kn_p.reshape(bp, tp, N_HEADS, 2, HEAD_DIM))
        new_v.append(proj[:n_p, 2 * ATTN_W:3 * ATTN_W].reshape(bp, tp, N_HEADS, V_DIM))

        qt_s, kk_s, vt_s = qkv_prep(proj, n_p, bs, ts, gmat, qg, kg, cos_t, sin_t,
                                    rope=True, emit_kn=False)
        ck = cache_k[:, l].reshape(bs, past, ATTN_W).astype(BF16)
        cv = cache_v[:, l].reshape(bs, past // KV_CHUNK, KV_CHUNK, N_HEADS, V_DIM).astype(BF16)
        cv = jnp.concatenate([cv.transpose(0, 1, 3, 4, 2),
                              jnp.ones((bs, past // KV_CHUNK, N_HEADS, V_ONES_ROWS, KV_CHUNK), BF16)], axis=3)
        cv = cv.reshape(bs, past // KV_CHUNK, N_HEADS * V_AUG, KV_CHUNK)
        attn_s = diff_attention(lam, qt_s, kk_s, vt_s, subln_g[l], lam_init, ctx=(ck, cv))

        pool = multiscale_pool(proj, tile_t0, tile_len, _block_diag(pool_w[l]).astype(BF16),
                               pool_scale[l])

        four_p = fourier_mix(dft_p[0], dft_p[1],
                             za[:n_p].reshape(bp, tp, FOUR_W), zb[:n_p].reshape(bp, tp, FOUR_W))
        four_s = fourier_mix(dft_s[0], dft_s[1],
                             za[n_p:].reshape(bs, ts, FOUR_W), zb[n_p:].reshape(bs, ts, FOUR_W))
        four_p, four_s = four_p.reshape(n_p, FOUR_W), four_s.reshape(n_s, FOUR_W)

        rw_pad = jnp.zeros((d, LANES), F32).at[:, :N_EXPERTS].set(router_w[l])
        rb_pad = jnp.full((1, LANES), -1e30, F32).at[0, :N_EXPERTS].set(router_b[l])
        x1, h2, pos, gp, counts = post_mixer(
            x, tile_mod, (attn_p, attn_s), pool, (four_p, four_s), w_out[l].astype(BF16), mods,
            norm2_g[l], rw_pad, rb_pad, tri, low)

        cnt = counts.reshape(n_tiles, 8, LANES)[:, 0, :N_EXPERTS].astype(jnp.int32)
        seg = (cnt + SEG_ALIGN - 1) // SEG_ALIGN * SEG_ALIGN
        off = jnp.cumsum(seg, axis=1) - seg
        rows_e = jnp.sum(seg, axis=0)
        padded = (rows_e + bm - 1) // bm * bm
        pad_end = jnp.cumsum(padded)
        pad_start = pad_end - padded
        gbase = pad_start[None, :] + jnp.cumsum(seg, axis=0) - seg
        nchunk = seg // COPY_ROWS[1]
        copy_ratio = COPY_ROWS[0] // COPY_ROWS[1]
        n_used = (pad_end[-1] // bm).astype(jnp.int32).reshape(1)
        tables = {
            "off": off.reshape(-1), "gbase": gbase.reshape(-1), "nchunk": nchunk.reshape(-1),
            "tbig": jnp.sum(nchunk // copy_ratio, axis=1), "tsmall": jnp.sum(nchunk % copy_ratio, axis=1),
            "npad": (padded - rows_e) // SEG_ALIGN,
            "padbase": pad_start + rows_e, "n_used": n_used,
        }
        block_row0 = jnp.arange(n_blocks, dtype=jnp.int32) * bm
        block_expert = jnp.minimum(
            jnp.sum((pad_end[None, :] <= block_row0[:, None]).astype(jnp.int32), axis=1), N_EXPERTS - 1)

        x_rows = moe_dispatch(h2, pos, tables, n_rows)
        next_first = (pad_end // bm)[block_expert]
        y_rows = expert_ffn(block_expert, next_first, n_used, x_rows, l, w_gate_up, b_gate_up, w_down,
                            b_down)
        x = moe_combine(x1, tile_mod, y_rows, gp, mods, tables, n_p)

    y_prompt = x[0].reshape(bp, tp, d)
    y_sample = x[1].reshape(bs, ts, d)
    return y_prompt, y_sample, jnp.stack(new_k, axis=1), jnp.stack(new_v, axis=1)
```

```python
import functools
import math

import numpy as np
import jax
import jax.numpy as jnp
from jax import lax
from jax.experimental import pallas as pl
from jax.experimental.pallas import tpu as pltpu

F32 = jnp.float32
BF16 = jnp.bfloat16

GRID_W = 64
HEAD_DIM = 64
V_DIM = 2 * HEAD_DIM
N_HEADS = 4
ATTN_W = N_HEADS * V_DIM
POOL_WINDOWS = (2, 4, 8, 16)
POOL_GROUP = 64
POOL_W = 256
FOUR_GROUPS = 4
FOUR_GROUP = 64
FOUR_W = 256
N_EXPERTS = 32
TOP_K = 4
SWIGLU_LIMIT = 7.0
SWIGLU_ALPHA = 1.702
ROPE_THETA = 10000.0
RMS_EPS = 1e-6

LANES = 128
TOKEN_TILE = 256
POOL_HALO = 8
KV_CHUNK = 256
ATTN_STEP_CHUNKS = 2
ATTN_Q_TILE = 512
ATTN_LOOKAHEAD = 2
V_ONES_ROWS = 16
V_AUG = V_DIM + V_ONES_ROWS
EXPERT_BLOCK = 256
SEG_ALIGN = 8
TILE_SLOTS = TOKEN_TILE * TOP_K + N_EXPERTS * SEG_ALIGN
COPY_ROWS = (4 * SEG_ALIGN, SEG_ALIGN)
DFT_ROW_TILE = 1024
MOD_ROWS = 16
VMEM_LIMIT = 56 * 1024 * 1024


def _dot(a, b):
    return jnp.dot(a, b, preferred_element_type=F32)


def _split_bf16(x):
    hi = x.astype(BF16)
    lo = (x - hi.astype(F32)).astype(BF16)
    return hi, lo


def _cparams(sem):
    return pltpu.CompilerParams(dimension_semantics=sem, vmem_limit_bytes=VMEM_LIMIT)


def _mod_kernel(cond_ref, w_ref, b_ref, o_ref):
    c = cond_ref[...]
    a = c * jax.nn.sigmoid(c)
    ah, al = _split_bf16(a)
    wh, wl = _split_bf16(w_ref[0])
    o_ref[0] = _dot(ah, wh) + _dot(ah, wl) + _dot(al, wh) + b_ref[0]


def modulation_all(cond, ada_w, ada_b):
    depth, d, d6 = ada_w.shape
    tn = d6 // 4
    return pl.pallas_call(
        _mod_kernel,
        out_shape=jax.ShapeDtypeStruct((depth, MOD_ROWS, d6), F32),
        grid=(depth, d6 // tn),
        in_specs=[
            pl.BlockSpec((MOD_ROWS, d), lambda l, j: (0, 0)),
            pl.BlockSpec((1, d, tn), lambda l, j: (l, 0, j)),
            pl.BlockSpec((1, 1, tn), lambda l, j: (l, 0, j)),
        ],
        out_specs=pl.BlockSpec((1, MOD_ROWS, tn), lambda l, j: (l, 0, j)),
        compiler_params=_cparams(("parallel", "parallel")),
        name="modulation",
    )(cond, ada_w, ada_b.reshape(depth, 1, d6))


def _rms_mod(x, g, scale, shift):
    ms = jnp.mean(x * x, axis=-1, keepdims=True)
    return x * lax.rsqrt(ms + RMS_EPS) * g * (1.0 + scale) + shift


def _pair_specs(block, first_tiles):
    return [pl.BlockSpec(block, lambda i, *_: (jnp.minimum(i, first_tiles - 1), 0)),
            pl.BlockSpec(block, lambda i, *_: (jnp.maximum(i - first_tiles, 0), 0))]


def _pick_group(first_ref, second_ref, first_tiles):
    return jnp.where(pl.program_id(0) < first_tiles, first_ref[...], second_ref[...])


def _pre_mixer_kernel(tmod_ref, xp_ref, xs_ref, g_ref, mod_ref, w_ref, ab_ref, proj_ref, za_ref, zb_ref,
                      *, first_tiles):
    del tmod_ref
    x = _pick_group(xp_ref, xs_ref, first_tiles)
    h = _rms_mod(x, g_ref[...], mod_ref[0, 1], mod_ref[0, 0])
    proj = _dot(h.astype(BF16), w_ref[...])
    nproj = proj_ref.shape[1]
    proj_ref[...] = proj[:, :nproj]
    z = _dot(proj[:, nproj:].astype(BF16), ab_ref[...])
    za_ref[...] = z[:, :FOUR_W].astype(BF16)
    zb_ref[...] = z[:, FOUR_W:].astype(BF16)


def pre_mixer(x, tile_mod, n1, mods, w_in_b, ab_b):
    d = x[0].shape[1]
    n = x[0].shape[0] + x[1].shape[0]
    in_w = w_in_b.shape[1]
    nproj = in_w - FOUR_W
    tm = TOKEN_TILE
    first_tiles = x[0].shape[0] // tm
    return pl.pallas_call(
        functools.partial(_pre_mixer_kernel, first_tiles=first_tiles),
        out_shape=(jax.ShapeDtypeStruct((n, nproj), F32),
                   jax.ShapeDtypeStruct((n, FOUR_W), BF16),
                   jax.ShapeDtypeStruct((n, FOUR_W), BF16)),
        grid_spec=pltpu.PrefetchScalarGridSpec(
            num_scalar_prefetch=1,
            grid=(n // tm,),
            in_specs=_pair_specs((tm, d), first_tiles) + [
                pl.BlockSpec((1, d), lambda i, tmod: (0, 0)),
                pl.BlockSpec((1, 6, 1, d), lambda i, tmod: (tmod[i], 0, 0, 0)),
                pl.BlockSpec((d, in_w), lambda i, tmod: (0, 0)),
                pl.BlockSpec((FOUR_W, 2 * FOUR_W), lambda i, tmod: (0, 0)),
            ],
            out_specs=(pl.BlockSpec((tm, nproj), lambda i, tmod: (i, 0)),
                       pl.BlockSpec((tm, FOUR_W), lambda i, tmod: (i, 0)),
                       pl.BlockSpec((tm, FOUR_W), lambda i, tmod: (i, 0))),
        ),
        compiler_params=_cparams(("parallel",)),
        name="pre_mixer",
    )(tile_mod, x[0], x[1], n1.reshape(1, d), mods, w_in_b, ab_b)


def _group_rms(x, gmat, g):
    ss = _dot((x * x).astype(BF16), gmat)
    return x * lax.rsqrt(ss * (1.0 / HEAD_DIM) + RMS_EPS) * g


def _rope(y, cos, sin_signed):
    w = y.shape[1]
    q = HEAD_DIM // 4
    lane = lax.broadcasted_iota(jnp.int32, y.shape, 1)
    partner = jnp.where(lane % (2 * q) < q, pltpu.roll(y, w - q, axis=1), pltpu.roll(y, q, axis=1))
    return y * cos + partner * sin_signed


def _qkv_kernel(q_ref, k_ref, v_ref, gmat_ref, qg_ref, kg_ref, *rest, rope, emit_kn):
    if rope:
        cos_ref, sin_ref, *outs = rest
    else:
        outs = rest
    qt_ref, kk_ref, vt_ref = outs[:3]
    gmat = gmat_ref[...]
    qn = _group_rms(q_ref[...], gmat, qg_ref[...])
    kn = _group_rms(k_ref[...], gmat, kg_ref[...])
    if emit_kn:
        outs[3][0] = kn
    if rope:
        cos = cos_ref[...]
        sin = sin_ref[...]
        qn = _rope(qn, cos, sin)
        kn = _rope(kn, cos, sin)
    qt_ref[0] = (qn * (HEAD_DIM ** -0.5 * math.log2(math.e))).T.astype(BF16)
    kk_ref[0] = kn.astype(BF16)
    vt = v_ref[...].T.astype(BF16)
    ones = jnp.ones((V_ONES_ROWS, vt.shape[1]), BF16)
    pieces = []
    for h in range(N_HEADS):
        pieces += [vt[h * V_DIM:(h + 1) * V_DIM], ones]
    vt_ref[0, 0] = jnp.concatenate(pieces, axis=0)


def qkv_prep(proj, row_off, b, t, gmat, qg, kg, cos, sin, *, rope, emit_kn):
    tm = TOKEN_TILE
    nt = t // tm
    off = row_off // tm
    w = ATTN_W
    row = lambda bi, j: (off + bi * nt + j)
    in_specs = [
        pl.BlockSpec((tm, w), lambda bi, j: (row(bi, j), 0)),
        pl.BlockSpec((tm, w), lambda bi, j: (row(bi, j), 1)),
        pl.BlockSpec((tm, w), lambda bi, j: (row(bi, j), 2)),
        pl.BlockSpec((w, w), lambda bi, j: (0, 0)),
        pl.BlockSpec((1, w), lambda bi, j: (0, 0)),
        pl.BlockSpec((1, w), lambda bi, j: (0, 0)),
    ]
    args = [proj, proj, proj, gmat, qg, kg]
    if rope:
        in_specs += [pl.BlockSpec((tm, w), lambda bi, j: (j, 0)),
                     pl.BlockSpec((tm, w), lambda bi, j: (j, 0))]
        args += [cos, sin]
    out_shape = [jax.ShapeDtypeStruct((b, w, t), BF16),
                 jax.ShapeDtypeStruct((b, t, w), BF16),
                 jax.ShapeDtypeStruct((b, nt, N_HEADS * V_AUG, tm), BF16)]
    out_specs = [pl.BlockSpec((1, w, tm), lambda bi, j: (bi, 0, j)),
                 pl.BlockSpec((1, tm, w), lambda bi, j: (bi, j, 0)),
                 pl.BlockSpec((1, 1, N_HEADS * V_AUG, tm), lambda bi, j: (bi, j, 0, 0))]
    if emit_kn:
        out_shape.append(jax.ShapeDtypeStruct((b, t, w), F32))
        out_specs.append(pl.BlockSpec((1, tm, w), lambda bi, j: (bi, j, 0)))
    return pl.pallas_call(
        functools.partial(_qkv_kernel, rope=rope, emit_kn=emit_kn),
        out_shape=tuple(out_shape),
        grid=(b, nt),
        in_specs=in_specs,
        out_specs=tuple(out_specs),
        compiler_params=_cparams(("parallel", "parallel")),
        name="qkv_prep_rope" if rope else "qkv_prep",
    )(*args)


def _attn_kernel(lam_ref, qt_ref, k_ref, vt_ref, *rest, n_own, n_ctx, sub, lam_init):
    if n_ctx:
        kc_ref, vc_ref, g_ref, o_ref = rest
    else:
        g_ref, o_ref = rest
    q = qt_ref[0]
    row = lax.broadcasted_iota(jnp.int32, q.shape, 0)
    zero = jnp.zeros_like(q)
    qmaps = (jnp.where(row < HEAD_DIM, q, zero), jnp.where(row >= HEAD_DIM, q, zero))
    tk = sub * KV_CHUNK
    steps = [(k_ref, vt_ref, c) for c in range(n_own)] + [(kc_ref, vc_ref, c) for c in range(n_ctx)]
    nk = len(steps)

    def scores(step):
        kr, _, c = steps[step]
        k = kr[0, c * tk:(c + 1) * tk, :]
        return [_dot(k, qm) for qm in qmaps]

    m_run, acc = [None, None], [None, None]
    pending = [scores(j) for j in range(min(ATTN_LOOKAHEAD, nk))]
    for c in range(nk):
        s_cur = pending.pop(0)
        if c + ATTN_LOOKAHEAD < nk:
            pending.append(scores(c + ATTN_LOOKAHEAD))
        _, vr, cv = steps[c]
        for m in range(2):
            s = s_cur[m]
            m_blk = jnp.max(s, axis=0, keepdims=True)
            m_new = m_blk if c == 0 else jnp.maximum(m_run[m], m_blk)
            pb = jnp.exp2(s - m_new).astype(BF16)
            pv = _dot(vr[0, cv * sub], pb[:KV_CHUNK])
            for j in range(1, sub):
                pv = pv + _dot(vr[0, cv * sub + j], pb[j * KV_CHUNK:(j + 1) * KV_CHUNK])
            if c == 0:
                acc[m] = pv
            else:
                acc[m] = jnp.exp2(m_run[m] - m_new) * acc[m] + pv
            m_run[m] = m_new

    outs = [a[:V_DIM] / a[V_DIM:V_DIM + 1] for a in acc]
    o = outs[0] - lam_ref[0] * outs[1]
    ms = jnp.mean(o * o, axis=0, keepdims=True)
    o = o * lax.rsqrt(ms + RMS_EPS) * g_ref[...] * (1.0 - lam_init)
    o_ref[...] = o.T.astype(BF16)


def diff_attention(lam, qt, kk, vt, subln_g, lam_init, ctx=None):
    b, w, t = qt.shape
    tq = min(ATTN_Q_TILE, t)
    nq = t // tq
    hw = V_DIM
    kv = [(kk, vt)] + ([ctx] if ctx is not None else [])
    chunks = [k.shape[1] // KV_CHUNK for k, _ in kv]
    sub = ATTN_STEP_CHUNKS if all(c % ATTN_STEP_CHUNKS == 0 for c in chunks) else 1
    in_specs = [pl.BlockSpec((1, hw, tq), lambda bi, h, qi, lam: (bi, h, qi))]
    args = [qt]
    for (k, v), nc in zip(kv, chunks):
        in_specs += [pl.BlockSpec((1, k.shape[1], hw), lambda bi, h, qi, lam: (bi, 0, h)),
                     pl.BlockSpec((1, nc, V_AUG, KV_CHUNK), lambda bi, h, qi, lam: (bi, 0, h, 0))]
        args += [k, v]
    in_specs.append(pl.BlockSpec((hw, 1), lambda bi, h, qi, lam: (0, 0)))
    return pl.pallas_call(
        functools.partial(_attn_kernel, n_own=chunks[0] // sub,
                          n_ctx=chunks[1] // sub if ctx is not None else 0, sub=sub, lam_init=lam_init),
        out_shape=jax.ShapeDtypeStruct((b * t, w), BF16),
        grid_spec=pltpu.PrefetchScalarGridSpec(
            num_scalar_prefetch=1,
            grid=(b, N_HEADS, nq),
            in_specs=in_specs,
            out_specs=pl.BlockSpec((tq, hw), lambda bi, h, qi, lam: (bi * nq + qi, h)),
        ),
        compiler_params=_cparams(("parallel", "parallel", "parallel")),
        name="diff_attention",
    )(lam, *args, subln_g.reshape(hw, 1))


def _pool_kernel(t0_ref, tlen_ref, prev_ref, cur_ref, next_ref, pw_ref, ps_ref, o_ref):
    i = pl.program_id(0)
    t0 = t0_ref[i]
    tlen = tlen_ref[i]
    tm = cur_ref.shape[0]
    u = cur_ref[...]
    prev = jnp.where(t0 > 0, prev_ref[...], 0.0)
    nxt = jnp.where(t0 + tm < tlen, next_ref[...], 0.0)
    e = jnp.concatenate([prev, u, nxt], axis=0)
    n = e.shape[0]
    w2 = e + pltpu.roll(e, 1, axis=0)
    w4 = pltpu.roll(w2, 1, axis=0) + pltpu.roll(w2, n - 1, axis=0)
    w8 = pltpu.roll(w4, 2, axis=0) + pltpu.roll(w4, n - 2, axis=0)
    w16 = pltpu.roll(w8, 4, axis=0) + pltpu.roll(w8, n - 4, axis=0)
    sums = [w[POOL_HALO:POOL_HALO + tm] for w in (w2, w4, w8, w16)]
    lane = lax.broadcasted_iota(jnp.int32, u.shape, 1)
    grp = lane // POOL_GROUP
    tabs = t0 + lax.broadcasted_iota(jnp.int32, u.shape, 0)
    total = sums[3]
    half = jnp.full(u.shape, POOL_WINDOWS[3] // 2, jnp.int32)
    for gi in (2, 1, 0):
        total = jnp.where(grp == gi, sums[gi], total)
        half = jnp.where(grp == gi, POOL_WINDOWS[gi] // 2, half)
    cnt = jnp.minimum(tabs + half, tlen) - jnp.maximum(tabs - half, 0)
    d = total / cnt.astype(F32) - u
    o_ref[...] = (_dot(d.astype(BF16), pw_ref[...]) * ps_ref[...]).astype(BF16)


def multiscale_pool(proj, tile_t0, tile_len, pw_b, pool_scale):
    n = proj.shape[0]
    tm = TOKEN_TILE
    hb = tm // POOL_HALO
    col = (3 * ATTN_W) // POOL_W
    last = n // POOL_HALO - 1
    return pl.pallas_call(
        _pool_kernel,
        out_shape=jax.ShapeDtypeStruct((n, POOL_W), BF16),
        grid_spec=pltpu.PrefetchScalarGridSpec(
            num_scalar_prefetch=2,
            grid=(n // tm,),
            in_specs=[
                pl.BlockSpec((POOL_HALO, POOL_W), lambda i, a, b: (jnp.maximum(i * hb - 1, 0), col)),
                pl.BlockSpec((tm, POOL_W), lambda i, a, b: (i, col)),
                pl.BlockSpec((POOL_HALO, POOL_W), lambda i, a, b: (jnp.minimum((i + 1) * hb, last), col)),
                pl.BlockSpec((POOL_W, POOL_W), lambda i, a, b: (0, 0)),
                pl.BlockSpec((1, POOL_W), lambda i, a, b: (0, 0)),
            ],
            out_specs=pl.BlockSpec((tm, POOL_W), lambda i, a, b: (i, 0)),
        ),
        compiler_params=_cparams(("parallel",)),
        name="multiscale_pool",
    )(tile_t0, tile_len, proj, proj, proj, pw_b, pool_scale.reshape(1, POOL_W))


def _fourier_kernel(c_ref, s_ref, za_ref, zb_ref, o_ref):
    o_ref[0] = (_dot(c_ref[...], za_ref[0]) - _dot(s_ref[...], zb_ref[0])).astype(BF16)


def fourier_mix(cmat, smat, za, zb):
    b, t, w = za.shape
    tr = min(DFT_ROW_TILE, t)
    return pl.pallas_call(
        _fourier_kernel,
        out_shape=jax.ShapeDtypeStruct((b, t, w), BF16),
        grid=(t // tr, b),
        in_specs=[
            pl.BlockSpec((tr, t), lambda r, bi: (r, 0)),
            pl.BlockSpec((tr, t), lambda r, bi: (r, 0)),
            pl.BlockSpec((1, t, w), lambda r, bi: (bi, 0, 0)),
            pl.BlockSpec((1, t, w), lambda r, bi: (bi, 0, 0)),
        ],
        out_specs=pl.BlockSpec((1, tr, w), lambda r, bi: (bi, r, 0)),
        compiler_params=_cparams(("parallel", "parallel")),
        name="fourier_mix",
    )(cmat, smat, za, zb)


def _post_mixer_kernel(tmod_ref, xp_ref, xs_ref, ap_ref, as_ref, p_ref, fp_ref, fs_ref, wo_ref, mod_ref,
                       g2_ref, rw_ref, rb_ref, tri_ref, low_ref, x1_ref, h2_ref, pos_ref, gp_ref, cnt_ref,
                       *, first_tiles):
    del tmod_ref
    attn = _pick_group(ap_ref, as_ref, first_tiles)
    four = _pick_group(fp_ref, fs_ref, first_tiles)
    mixed = jnp.concatenate([attn, p_ref[...], four], axis=1)
    x1 = _pick_group(xp_ref, xs_ref, first_tiles) + mod_ref[0, 2] * _dot(mixed, wo_ref[...])
    x1_ref[...] = x1
    h2 = _rms_mod(x1, g2_ref[...], mod_ref[0, 4], mod_ref[0, 3])
    hh, hl = _split_bf16(h2)
    h2_ref[...] = hh

    rw = rw_ref[...]
    rwh, rwl = _split_bf16(rw)
    logits = _dot(hh, rwh) + _dot(hh, rwl) + _dot(hl, rwh) + rb_ref[...]
    lt = logits.T[:N_EXPERTS]
    eidx = lax.broadcasted_iota(jnp.int32, lt.shape, 0).astype(F32)
    work = lt
    vals, hots = [], []
    for _ in range(TOP_K):
        mx = jnp.max(work, axis=0, keepdims=True)
        ix = jnp.min(jnp.where(work == mx, eidx, float(N_EXPERTS)), axis=0, keepdims=True)
        hot = eidx == ix
        work = jnp.where(hot, -jnp.inf, work)
        vals.append(mx)
        hots.append(hot)
    exps = [jnp.exp(v - vals[0]) for v in vals]
    denom = exps[0] + exps[1] + exps[2] + exps[3]
    sel = (hots[0] | hots[1] | hots[2] | hots[3]).astype(F32)
    selb = sel.astype(BF16)
    prefix = _dot(selb, tri_ref[...])
    cnt_col = jnp.sum(sel, axis=1, keepdims=True)
    seg_col = jnp.ceil(cnt_col * (1.0 / SEG_ALIGN)) * SEG_ALIGN
    seg_start = _dot(low_ref[...], jnp.broadcast_to(seg_col, (N_EXPERTS, LANES)).astype(BF16))[:, 0:1]
    slot = prefix + seg_start
    tm = lt.shape[1]
    pos = [jnp.sum(jnp.where(h, slot, 0.0), axis=0, keepdims=True) for h in hots]
    pos_ref[...] = jnp.concatenate(pos + [jnp.zeros((8 - TOP_K, tm), F32)], axis=0).astype(jnp.int32)
    gp_t = jnp.concatenate([e / denom for e in exps] + pos
                           + [jnp.zeros((LANES - 2 * TOP_K, tm), F32)], axis=0)
    gp_ref[...] = gp_t.T
    sel_pad = jnp.concatenate([selb, jnp.zeros((LANES - N_EXPERTS, tm), BF16)], axis=0)
    cnt_ref[...] = lax.dot_general(jnp.ones((8, tm), BF16), sel_pad, (((1,), (1,)), ((), ())),
                                   preferred_element_type=F32)


def post_mixer(x, tile_mod, attn, pool, four, wo_b, mods, n2, rw_pad, rb_pad, tri, low):
    d = x[0].shape[1]
    n = x[0].shape[0] + x[1].shape[0]
    tm = TOKEN_TILE
    first_tiles = attn[0].shape[0] // tm
    tok = lambda i, tmod: (i, 0)
    const = lambda i, tmod: (0, 0)
    lane_tok = lambda i, tmod: (0, i)
    return pl.pallas_call(
        functools.partial(_post_mixer_kernel, first_tiles=first_tiles),
        out_shape=(jax.ShapeDtypeStruct((n, d), F32),
                   jax.ShapeDtypeStruct((n, d), BF16),
                   jax.ShapeDtypeStruct((8, n), jnp.int32),
                   jax.ShapeDtypeStruct((n, LANES), F32),
                   jax.ShapeDtypeStruct((n // tm * 8, LANES), F32)),
        grid_spec=pltpu.PrefetchScalarGridSpec(
            num_scalar_prefetch=1,
            grid=(n // tm,),
            in_specs=_pair_specs((tm, d), first_tiles) + _pair_specs((tm, ATTN_W), first_tiles) + [
                pl.BlockSpec((tm, POOL_W), tok),
            ] + _pair_specs((tm, FOUR_W), first_tiles) + [
                pl.BlockSpec((d, d), const),
                pl.BlockSpec((1, 6, 1, d), lambda i, tmod: (tmod[i], 0, 0, 0)),
                pl.BlockSpec((1, d), const),
                pl.BlockSpec((d, LANES), const),
                pl.BlockSpec((1, LANES), const),
                pl.BlockSpec((tm, tm), const),
                pl.BlockSpec((N_EXPERTS, N_EXPERTS), const),
            ],
            out_specs=(pl.BlockSpec((tm, d), tok),
                       pl.BlockSpec((tm, d), tok),
                       pl.BlockSpec((8, tm), lane_tok),
                       pl.BlockSpec((tm, LANES), tok),
                       pl.BlockSpec((8, LANES), tok)),
        ),
        compiler_params=_cparams(("parallel",)),
        name="post_mixer_router",
    )(tile_mod, x[0], x[1], attn[0], attn[1], pool, four[0], four[1], wo_b, mods, n2.reshape(1, d),
      rw_pad, rb_pad, tri, low)


def _slot_copy_loops(i, off_ref, gbase_ref, nchunk_ref, start_copy):
    ratio = COPY_ROWS[0] // COPY_ROWS[1]

    def per_expert(e, carry):
        idx = i * N_EXPERTS + e
        local0 = off_ref[idx]
        global0 = gbase_ref[idx]
        n_small_units = nchunk_ref[idx]
        n_big = n_small_units // ratio
        rest0 = n_big * COPY_ROWS[0]

        def big(j, c):
            start_copy(pl.multiple_of(local0 + j * COPY_ROWS[0], SEG_ALIGN),
                       pl.multiple_of(global0 + j * COPY_ROWS[0], SEG_ALIGN), 0)
            return c

        def small(j, c):
            start_copy(pl.multiple_of(local0 + rest0 + j * COPY_ROWS[1], SEG_ALIGN),
                       pl.multiple_of(global0 + rest0 + j * COPY_ROWS[1], SEG_ALIGN), 1)
            return c

        n_small = n_small_units - n_big * ratio

        @pl.when(n_big >= 1)
        def _():
            big(0, 0)

        lax.fori_loop(1, n_big, big, 0)
        for j in range(ratio - 1):
            @pl.when(j < n_small)
            def _():
                small(j, 0)
        return carry

    lax.fori_loop(0, N_EXPERTS, per_expert, 0)


def _wait_copies(copy, count_big, count_small):
    def wait_big(j, c):
        copy(0, 0, 0).wait()
        return c

    def wait_small(j, c):
        copy(0, 0, 1).wait()
        return c

    lax.fori_loop(0, count_big, wait_big, 0)
    lax.fori_loop(0, count_small, wait_small, 0)


def _dispatch_kernel(off_ref, gbase_ref, nchunk_ref, tbig_ref, tsmall_ref, npad_ref, padbase_ref,
                     nused_ref, h2_ref, pos_ref, xrows_ref, xs, zbuf, sem):
    i = pl.program_id(0)
    last = pl.num_programs(0) - 1
    slot = i % 2
    pos = pos_ref[...]
    rows = lax.broadcasted_iota(jnp.int32, (xs.shape[1], pos.shape[1]), 0)
    hit = rows == pos[0:1]
    for k in range(1, TOP_K):
        hit = hit | (rows == pos[k:k + 1])
    xs[slot] = _dot(hit.astype(BF16), h2_ref[...])

    def seg_copy(s, local, glob, size):
        rows = COPY_ROWS[size]
        return pltpu.make_async_copy(xs.at[s, pl.ds(local, rows)],
                                     xrows_ref.at[pl.ds(glob, rows)], sem.at[s, size])

    _slot_copy_loops(i, off_ref, gbase_ref, nchunk_ref,
                     lambda local, glob, size: seg_copy(slot, local, glob, size).start())

    @pl.when(i > 0)
    def _():
        prev = jnp.maximum(i - 1, 0)
        _wait_copies(functools.partial(seg_copy, 1 - slot), tbig_ref[prev], tsmall_ref[prev])

    @pl.when(i == last)
    def _():
        _wait_copies(functools.partial(seg_copy, slot), tbig_ref[i], tsmall_ref[i])
        zbuf[...] = jnp.zeros(zbuf.shape, zbuf.dtype)

        def zero_copy(dst, rows):
            return pltpu.make_async_copy(zbuf.at[pl.ds(0, rows)], xrows_ref.at[pl.ds(dst, rows)],
                                         sem.at[slot, 0])

        def per_expert(e, total):
            def per_chunk(j, c):
                zero_copy(pl.multiple_of(padbase_ref[e] + j * SEG_ALIGN, SEG_ALIGN), SEG_ALIGN).start()
                return c
            lax.fori_loop(0, npad_ref[e], per_chunk, 0)
            return total + npad_ref[e]

        total = lax.fori_loop(0, N_EXPERTS, per_expert, 0)

        def wait_zero(j, c):
            zero_copy(0, SEG_ALIGN).wait()
            return c

        lax.fori_loop(0, total, wait_zero, 0)

        bm = zbuf.shape[0]
        n_blocks = xrows_ref.shape[0] // bm

        def zero_block(b, c):
            zero_copy(pl.multiple_of(b * bm, bm), bm).start()
            return c

        def wait_block(b, c):
            zero_copy(0, bm).wait()
            return c

        lax.fori_loop(nused_ref[0], n_blocks, zero_block, 0)
        lax.fori_loop(nused_ref[0], n_blocks, wait_block, 0)


def moe_dispatch(h2, pos, tables, n_rows):
    n, d = h2.shape
    tm = TOKEN_TILE
    return pl.pallas_call(
        _dispatch_kernel,
        out_shape=jax.ShapeDtypeStruct((n_rows, d), F32),
        grid_spec=pltpu.PrefetchScalarGridSpec(
            num_scalar_prefetch=8,
            grid=(n // tm,),
            in_specs=[
                pl.BlockSpec((tm, d), lambda i, *_: (i, 0)),
                pl.BlockSpec((8, tm), lambda i, *_: (0, i)),
            ],
            out_specs=pl.BlockSpec(memory_space=pl.ANY),
            scratch_shapes=[pltpu.VMEM((2, TILE_SLOTS, d), F32),
                            pltpu.VMEM((EXPERT_BLOCK, d), F32),
                            pltpu.SemaphoreType.DMA((2, 2))],
        ),
        compiler_params=_cparams(("arbitrary",)),
        name="moe_dispatch",
    )(tables["off"], tables["gbase"], tables["nchunk"], tables["tbig"], tables["tsmall"],
      tables["npad"], tables["padbase"], tables["n_used"], h2, pos)


def _expert_kernel(be_ref, nxt_ref, nused_ref, x_ref, wgu_ref, bgu_ref, wd_ref, bd_ref, y_ref,
                   stage_gu, stage_d, wgu_s, wd_s, sem, *, layer):
    b = pl.program_id(0)
    n_used = nused_ref[0]

    def weight_copies(e):
        return (pltpu.make_async_copy(wgu_ref.at[layer, e], stage_gu, sem.at[0]),
                pltpu.make_async_copy(wd_ref.at[layer, e], stage_d, sem.at[1]))

    first = ((b == 0) | (be_ref[b] != be_ref[jnp.maximum(b - 1, 0)])) & (b < n_used)

    @pl.when(first)
    def _():
        @pl.when(b == 0)
        def _():
            for c in weight_copies(be_ref[0]):
                c.start()

        for c in weight_copies(be_ref[b]):
            c.wait()
        wgu_s[...] = stage_gu[...].astype(BF16)
        wd_s[...] = stage_d[...].astype(BF16)
        nb = nxt_ref[b]

        @pl.when(nb < n_used)
        def _():
            for c in weight_copies(be_ref[jnp.minimum(nb, pl.num_programs(0) - 1)]):
                c.start()

    @pl.when(b < nused_ref[0])
    def _():
        gu = _dot(x_ref[...].astype(BF16), wgu_s[...]) + bgu_ref[0]
        f = gu.shape[1] // 2
        glu = jnp.minimum(gu[:, :f], SWIGLU_LIMIT)
        lin = jnp.clip(gu[:, f:], -SWIGLU_LIMIT, SWIGLU_LIMIT)
        act = glu * jax.nn.sigmoid(SWIGLU_ALPHA * glu) * (lin + 1.0)
        y_ref[...] = _dot(act.astype(BF16), wd_s[...]) + bd_ref[0]

    @pl.when(b >= nused_ref[0])
    def _():
        y_ref[...] = jnp.zeros(y_ref.shape, y_ref.dtype)


def expert_ffn(block_expert, next_first, n_used, x_rows, layer, wgu, bgu, wd, bd):
    n_rows, d = x_rows.shape
    depth, e, _, f2 = wgu.shape
    bm = EXPERT_BLOCK
    return pl.pallas_call(
        functools.partial(_expert_kernel, layer=layer),
        out_shape=jax.ShapeDtypeStruct((n_rows, d), F32),
        grid_spec=pltpu.PrefetchScalarGridSpec(
            num_scalar_prefetch=3,
            grid=(n_rows // bm,),
            in_specs=[
                pl.BlockSpec((bm, d), lambda b, be, nx, nu: (jnp.minimum(b, nu[0] - 1), 0)),
                pl.BlockSpec(memory_space=pl.ANY),
                pl.BlockSpec((1, 1, f2), lambda b, be, nx, nu: (layer * e + be[b], 0, 0)),
                pl.BlockSpec(memory_space=pl.ANY),
                pl.BlockSpec((1, 1, d), lambda b, be, nx, nu: (layer * e + be[b], 0, 0)),
            ],
            out_specs=pl.BlockSpec((bm, d), lambda b, be, nx, nu: (b, 0)),
            scratch_shapes=[pltpu.VMEM((d, f2), F32), pltpu.VMEM((f2 // 2, d), F32),
                            pltpu.VMEM((d, f2), BF16), pltpu.VMEM((f2 // 2, d), BF16),
                            pltpu.SemaphoreType.DMA((2,))],
        ),
        compiler_params=_cparams(("arbitrary",)),
        name="expert_ffn",
    )(block_expert, next_first, n_used, x_rows, wgu, bgu.reshape(depth * e, 1, f2), wd,
      bd.reshape(depth * e, 1, d))


def _combine_kernel(tmod_ref, off_ref, gbase_ref, nchunk_ref, tbig_ref, tsmall_ref,
                    x_ref, gp_ref, mod_ref, yrows_ref, op_ref, os_ref, ys, sem, *, first_tiles):
    del tmod_ref
    i = pl.program_id(0)
    last = pl.num_programs(0) - 1
    slot = i % 2

    def seg_copy(s, local, glob, size):
        rows = COPY_ROWS[size]
        return pltpu.make_async_copy(yrows_ref.at[pl.ds(glob, rows)],
                                     ys.at[s, pl.ds(local, rows)], sem.at[s, size])

    def fetch(tile, s):
        _slot_copy_loops(tile, off_ref, gbase_ref, nchunk_ref,
                         lambda local, glob, size: seg_copy(s, local, glob, size).start())

    @pl.when(i == 0)
    def _():
        ys[...] = jnp.zeros(ys.shape, ys.dtype)
        fetch(0, 0)

    @pl.when(i < last)
    def _():
        fetch(i + 1, 1 - slot)

    _wait_copies(functools.partial(seg_copy, slot), tbig_ref[i], tsmall_ref[i])

    gp = gp_ref[...]
    lanes = lax.broadcasted_iota(jnp.int32, (gp.shape[0], ys.shape[1]), 1).astype(F32)
    wmat = jnp.where(lanes == gp[:, TOP_K:TOP_K + 1], gp[:, 0:1], 0.0)
    for k in range(1, TOP_K):
        wmat = jnp.where(lanes == gp[:, TOP_K + k:TOP_K + k + 1], gp[:, k:k + 1], wmat)
    x2 = x_ref[...] + mod_ref[0, 5] * _dot(wmat.astype(BF16), ys[slot].astype(BF16))

    @pl.when(i < first_tiles)
    def _():
        op_ref[...] = x2

    @pl.when(i >= first_tiles)
    def _():
        os_ref[...] = x2


def moe_combine(x1, tile_mod, y_rows, gp, mods, tables, first_rows):
    n, d = x1.shape
    tm = TOKEN_TILE
    first_tiles = first_rows // tm
    return pl.pallas_call(
        functools.partial(_combine_kernel, first_tiles=first_tiles),
        out_shape=(jax.ShapeDtypeStruct((first_rows, d), F32),
                   jax.ShapeDtypeStruct((n - first_rows, d), F32)),
        grid_spec=pltpu.PrefetchScalarGridSpec(
            num_scalar_prefetch=6,
            grid=(n // tm,),
            in_specs=[
                pl.BlockSpec((tm, d), lambda i, *_: (i, 0)),
                pl.BlockSpec((tm, LANES), lambda i, *_: (i, 0)),
                pl.BlockSpec((1, 6, 1, d), lambda i, tmod, *_: (tmod[i], 0, 0, 0)),
                pl.BlockSpec(memory_space=pl.ANY),
            ],
            out_specs=tuple(_pair_specs((tm, d), first_tiles)),
            scratch_shapes=[pltpu.VMEM((2, TILE_SLOTS, d), F32),
                            pltpu.SemaphoreType.DMA((2, 2))],
        ),
        compiler_params=_cparams(("arbitrary",)),
        name="moe_combine",
    )(tile_mod, tables["off"], tables["gbase"], tables["nchunk"], tables["tbig"], tables["tsmall"],
      x1, gp, mods, y_rows)


def _rope_tables(t):
    quarter = HEAD_DIM // 4
    lane = jnp.arange(ATTN_W)
    inv_freq = ROPE_THETA ** (-(lane % quarter).astype(F32) / quarter)
    pos_t = jnp.arange(t)
    pos = jnp.where((lane % HEAD_DIM)[None, :] < HEAD_DIM // 2,
                    (pos_t // GRID_W)[:, None], (pos_t % GRID_W)[:, None]).astype(F32)
    ang = pos * inv_freq[None, :]
    sign = jnp.where(lane % (2 * quarter) < quarter, -1.0, 1.0).astype(F32)
    return jnp.cos(ang), jnp.sin(ang) * sign[None, :]


def _dft_tables(t):
    scale = 1.0 / math.sqrt(t)
    pos = jnp.arange(t, dtype=jnp.int32)

    def angles(freq, period):
        return ((freq[:, None] * pos[None, :]) % period).astype(F32) * (2.0 * math.pi / period)

    r = math.isqrt(t)
    if r * r != t:
        ang = angles(pos, t)
        return (jnp.cos(ang) * scale).astype(BF16), (jnp.sin(ang) * scale).astype(BF16)
    sub = jnp.arange(r, dtype=jnp.int32)
    hi, lo = angles(sub, r), angles(sub, t)
    ch, sh = jnp.cos(hi)[:, None, :] * scale, jnp.sin(hi)[:, None, :] * scale
    cl, sl = jnp.cos(lo)[None, :, :], jnp.sin(lo)[None, :, :]
    return ((ch * cl - sh * sl).reshape(t, t).astype(BF16),
            (sh * cl + ch * sl).reshape(t, t).astype(BF16))


def _block_diag(blocks):
    g, c, d = blocks.shape
    eye = jnp.eye(g, dtype=blocks.dtype)
    return (eye[:, None, :, None] * blocks[:, :, None, :]).reshape(g * c, g * d)


def _fourier_weights(fourier_w_l):
    c = jnp.arange(FOUR_GROUP, dtype=jnp.int32)
    ang = ((c[:, None] * c[None, :]) % FOUR_GROUP).astype(F32) * (2.0 * math.pi / FOUR_GROUP)
    scale = 1.0 / math.sqrt(FOUR_GROUP)
    hp = lax.Precision.HIGHEST
    a = jnp.einsum("ck,gkd->gcd", jnp.cos(ang) * scale, fourier_w_l, precision=hp)
    b = jnp.einsum("ck,gkd->gcd", jnp.sin(ang) * scale, fourier_w_l, precision=hp)
    return jnp.concatenate([_block_diag(a), _block_diag(b)], axis=1)


def kernel(x_prompt, x_sample, c, c_ctx, cache_k, cache_v, ada_w, ada_b, norm1_g, norm2_g, w_in,
           q_norm_g, k_norm_g, lambda_qk, subln_g, pool_w, pool_scale, fourier_w, w_out,
           router_w, router_b, w_gate_up, b_gate_up, w_down, b_down):
    bp, tp, d = x_prompt.shape
    bs, ts, _ = x_sample.shape
    depth = ada_w.shape[0]
    past = cache_k.shape[2]
    tm = TOKEN_TILE
    n_p, n_s = bp * tp, bs * ts
    n = n_p + n_s
    assert tp % tm == 0 and ts % tm == 0 and past % KV_CHUNK == 0 and bs + 1 <= MOD_ROWS
    assert ts % min(ATTN_Q_TILE, ts) == 0 and tp % min(ATTN_Q_TILE, tp) == 0

    tile_mod = np.concatenate([np.zeros(n_p // tm, np.int32),
                               1 + np.repeat(np.arange(bs, dtype=np.int32), ts // tm)])
    tile_t0 = np.concatenate([np.tile(np.arange(0, tp, tm, dtype=np.int32), bp),
                              np.tile(np.arange(0, ts, tm, dtype=np.int32), bs)])
    tile_len = np.concatenate([np.full(n_p // tm, tp, np.int32), np.full(n_s // tm, ts, np.int32)])
    tile_mod, tile_t0, tile_len = map(jnp.asarray, (tile_mod, tile_t0, tile_len))

    cond = jnp.zeros((MOD_ROWS, d), F32).at[0].set(c_ctx).at[1:1 + bs].set(c)
    mods_all = modulation_all(cond, ada_w, ada_b).reshape(depth, MOD_ROWS, 6, 1, d)

    cos_t, sin_t = _rope_tables(ts)
    dft_p = _dft_tables(tp)
    dft_s = _dft_tables(ts)
    gmat = _block_diag(jnp.ones((ATTN_W // HEAD_DIM, HEAD_DIM, HEAD_DIM), BF16))
    tri = (jnp.arange(tm)[:, None] < jnp.arange(tm)[None, :]).astype(BF16)
    tile_w = ATTN_W // HEAD_DIM

    low = (jnp.arange(N_EXPERTS)[:, None] > jnp.arange(N_EXPERTS)[None, :]).astype(BF16)
    n_tiles = n // tm
    bm = EXPERT_BLOCK
    max_rows = n * TOP_K + n_tiles * N_EXPERTS * (SEG_ALIGN - 1)
    n_blocks = -(-max_rows // bm) + N_EXPERTS
    n_rows = n_blocks * bm

    x = (x_prompt.reshape(n_p, d), x_sample.reshape(n_s, d))
    new_k, new_v = [], []
    for l in range(depth):
        lam_init = 0.8 - 0.6 * math.exp(-0.3 * l)
        mods = mods_all[l]
        lf = lambda_qk[l]
        lam = (jnp.exp(jnp.sum(lf[0] * lf[1])) - jnp.exp(jnp.sum(lf[2] * lf[3])) + lam_init).reshape(1)

        ab = _fourier_weights(fourier_w[l]).astype(BF16)
        proj, za, zb = pre_mixer(x, tile_mod, norm1_g[l], mods, w_in[l].astype(BF16), ab)

        qg = jnp.tile(q_norm_g[l], tile_w).reshape(1, ATTN_W)
        kg = jnp.tile(k_norm_g[l], tile_w).reshape(1, ATTN_W)

        qt_p, kk_p, vt_p, kn_p = qkv_prep(proj, 0, bp, tp, gmat, qg, kg, None, None,
                                          rope=False, emit_kn=True)
        attn_p = diff_attention(lam, qt_p, kk_p, vt_p, subln_g[l], lam_init)
        new_k.append(kn_p.reshape(bp, tp, N_HEADS, 2, HEAD_DIM))
        new_v.append(proj[:n_p, 2 * ATTN_W:3 * ATTN_W].reshape(bp, tp, N_HEADS, V_DIM))

        qt_s, kk_s, vt_s = qkv_prep(proj, n_p, bs, ts, gmat, qg, kg, cos_t, sin_t,
                                    rope=True, emit_kn=False)
        ck = cache_k[:, l].reshape(bs, past, ATTN_W).astype(BF16)
        cv = cache_v[:, l].reshape(bs, past // KV_CHUNK, KV_CHUNK, N_HEADS, V_DIM).astype(BF16)
        cv = jnp.concatenate([cv.transpose(0, 1, 3, 4, 2),
                              jnp.ones((bs, past // KV_CHUNK, N_HEADS, V_ONES_ROWS, KV_CHUNK), BF16)], axis=3)
        cv = cv.reshape(bs, past // KV_CHUNK, N_HEADS * V_AUG, KV_CHUNK)
        attn_s = diff_attention(lam, qt_s, kk_s, vt_s, subln_g[l], lam_init, ctx=(ck, cv))

        pool = multiscale_pool(proj, tile_t0, tile_len, _block_diag(pool_w[l]).astype(BF16),
                               pool_scale[l])

        four_p = fourier_mix(dft_p[0], dft_p[1],
                             za[:n_p].reshape(bp, tp, FOUR_W), zb[:n_p].reshape(bp, tp, FOUR_W))
        four_s = fourier_mix(dft_s[0], dft_s[1],
                             za[n_p:].reshape(bs, ts, FOUR_W), zb[n_p:].reshape(bs, ts, FOUR_W))
        four_p, four_s = four_p.reshape(n_p, FOUR_W), four_s.reshape(n_s, FOUR_W)

        rw_pad = jnp.zeros((d, LANES), F32).at[:, :N_EXPERTS].set(router_w[l])
        rb_pad = jnp.full((1, LANES), -1e30, F32).at[0, :N_EXPERTS].set(router_b[l])
        x1, h2, pos, gp, counts = post_mixer(
            x, tile_mod, (attn_p, attn_s), pool, (four_p, four_s), w_out[l].astype(BF16), mods,
            norm2_g[l], rw_pad, rb_pad, tri, low)

        cnt = counts.reshape(n_tiles, 8, LANES)[:, 0, :N_EXPERTS].astype(jnp.int32)
        seg = (cnt + SEG_ALIGN - 1) // SEG_ALIGN * SEG_ALIGN
        off = jnp.cumsum(seg, axis=1) - seg
        rows_e = jnp.sum(seg, axis=0)
        padded = (rows_e + bm - 1) // bm * bm
        pad_end = jnp.cumsum(padded)
        pad_start = pad_end - padded
        gbase = pad_start[None, :] + jnp.cumsum(seg, axis=0) - seg
        nchunk = seg // COPY_ROWS[1]
        copy_ratio = COPY_ROWS[0] // COPY_ROWS[1]
        n_used = (pad_end[-1] // bm).astype(jnp.int32).reshape(1)
        tables = {
            "off": off.reshape(-1), "gbase": gbase.reshape(-1), "nchunk": nchunk.reshape(-1),
            "tbig": jnp.sum(nchunk // copy_ratio, axis=1), "tsmall": jnp.sum(nchunk % copy_ratio, axis=1),
            "npad": (padded - rows_e) // SEG_ALIGN,
            "padbase": pad_start + rows_e, "n_used": n_used,
        }
        block_row0 = jnp.arange(n_blocks, dtype=jnp.int32) * bm
        block_expert = jnp.minimum(
            jnp.sum((pad_end[None, :] <= block_row0[:, None]).astype(jnp.int32), axis=1), N_EXPERTS - 1)

        x_rows = moe_dispatch(h2, pos, tables, n_rows)
        next_first = (pad_end // bm)[block_expert]
        y_rows = expert_ffn(block_expert, next_first, n_used, x_rows, l, w_gate_up, b_gate_up, w_down,
                            b_down)
        x = moe_combine(x1, tile_mod, y_rows, gp, mods, tables, n_p)

    y_prompt = x[0].reshape(bp, tp, d)
    y_sample = x[1].reshape(bs, ts, d)
    return y_prompt, y_sample, jnp.stack(new_k, axis=1), jnp.stack(new_v, axis=1)
```
